```python
import math
import jax
import jax.numpy as jnp
from jax import lax
import numpy as np

D_MODEL = 1024
BATCH = 16
SEQ = 2048
DEPTH = 2

N_BRANCH = 4
BRANCH = 512
ROPE_THETA = 500000.0
EPS = 1e-6
Q_BLOCK = 128

DA_HEADS = 4
DA_DHALF = 64
DA_DV = 2 * DA_DHALF
DA_ROT = DA_DHALF // 4

MB_HEADDIM = 64
MB_HEADS = BRANCH // MB_HEADDIM
MB_GROUPS = 2
MB_PER_GROUP = MB_HEADS // MB_GROUPS
MB_STATE = 128
MB_CONV = 4
MB_CHUNK = 128
MB_CONV_DIM = BRANCH + 2 * MB_GROUPS * MB_STATE

S5_GROUP = 16
S5_GROUPS = BRANCH // S5_GROUP
S5_STATE = 64

MLA_HEADS = 4
MLA_Q_RANK = 256
MLA_KV_RANK = 128
MLA_NOPE = 128
MLA_ROPE = 64
MLA_V = BRANCH // MLA_HEADS
MLA_QK = MLA_NOPE + MLA_ROPE

IN_LAYOUT = (
    ('da_q', DA_HEADS * 2 * DA_DHALF),
    ('da_k', DA_HEADS * 2 * DA_DHALF),
    ('da_v', DA_HEADS * DA_DV),
    ('da_gate', BRANCH),
    ('mb_z', BRANCH),
    ('mb_xbc', MB_CONV_DIM),
    ('mb_dt', MB_HEADS),
    ('s5_u', BRANCH),
    ('s5_gate', BRANCH),
    ('mla_cq', MLA_Q_RANK),
    ('mla_ckv', MLA_KV_RANK),
    ('mla_krope', MLA_ROPE),
    ('mla_gate', BRANCH),
    ('gate_a', D_MODEL),
    ('gate_b', D_MODEL),
    ('gate_c', D_MODEL),
    ('gate_d', D_MODEL),
)
D_IN = sum(n for _, n in IN_LAYOUT)

kernel_name = 'hybrid_gated_diffattn_ssd_s5_mla'


def in_proj(h, w, name):
    start = 0
    for nm, n in IN_LAYOUT:
        if nm == name:
            return h @ w[:, start:start + n]
        start += n
    raise KeyError(name)


def rmsnorm(x, g):
    xf = x.astype(jnp.float32)
    y = xf * lax.rsqrt(jnp.mean(xf * xf, axis=-1, keepdims=True) + EPS)
    return (y * g.astype(jnp.float32)).astype(x.dtype)


def rope_tables(positions, rot_dim, dtype):
    inv_freq = 1.0 / (ROPE_THETA ** (jnp.arange(0, rot_dim, 2, dtype=jnp.float32) / rot_dim))
    ang = positions.astype(jnp.float32)[..., None] * inv_freq
    return jnp.cos(ang).astype(dtype), jnp.sin(ang).astype(dtype)


def rotate(x, cos, sin):
    half = x.shape[-1] // 2
    x1, x2 = x[..., :half], x[..., half:]
    return jnp.concatenate([x1 * cos - x2 * sin, x2 * cos + x1 * sin], axis=-1)


def causal_block_attention(q, k, v, scale):
    bsz, s = q.shape[0], q.shape[1]
    nb = s // Q_BLOCK
    qb = q.reshape((bsz, nb, Q_BLOCK) + q.shape[2:]).swapaxes(0, 1)
    kpos = jnp.arange(s)

    def one_block(args):
        qi, i = args
        sc = jnp.einsum('bqhmd,bkhmd->bhmqk', qi, k).astype(jnp.float32) * scale
        qpos = i * Q_BLOCK + jnp.arange(Q_BLOCK)
        sc = jnp.where(kpos[None, :] <= qpos[:, None], sc, -jnp.inf)
        p = jax.nn.softmax(sc, axis=-1).astype(v.dtype)
        return jnp.einsum('bhmqk,bkhe->bqhme', p, v)

    out = lax.map(one_block, (qb, jnp.arange(nb)))
    return out.swapaxes(0, 1).reshape((bsz, s) + out.shape[3:])


def diff_attention(q, k, v, gate, positions, q_g, k_g, lq1, lk1, lq2, lk2, subln_g, lambda_init):
    bsz, s = q.shape[0], q.shape[1]
    f32 = jnp.float32
    q = rmsnorm(q.reshape(bsz, s, DA_HEADS, 2, DA_DHALF), q_g)
    k = rmsnorm(k.reshape(bsz, s, DA_HEADS, 2, DA_DHALF), k_g)
    cos, sin = rope_tables(positions, DA_ROT, q.dtype)
    cos, sin = cos[:, :, None, None, :], sin[:, :, None, None, :]
    q = jnp.concatenate([rotate(q[..., :DA_ROT], cos, sin), q[..., DA_ROT:]], axis=-1)
    k = jnp.concatenate([rotate(k[..., :DA_ROT], cos, sin), k[..., DA_ROT:]], axis=-1)
    v = v.reshape(bsz, s, DA_HEADS, DA_DV)
    o = causal_block_attention(q, k, v, DA_DHALF ** -0.5)
    lam = (jnp.exp(jnp.sum(lq1.astype(f32) * lk1.astype(f32)))
           - jnp.exp(jnp.sum(lq2.astype(f32) * lk2.astype(f32))) + lambda_init)
    o = o[..., 0, :] - lam.astype(o.dtype) * o[..., 1, :]
    o = rmsnorm(o, subln_g) * (1.0 - lambda_init)
    return o.reshape(bsz, s, BRANCH) * jax.nn.silu(gate)


def causal_depthwise_conv(x, w, b):
    c = x.shape[-1]
    y = lax.conv_general_dilated(x, w[:, None, :].astype(x.dtype), window_strides=(1,),
                                 padding=[(MB_CONV - 1, 0)],
                                 dimension_numbers=('NWC', 'WIO', 'NWC'),
                                 feature_group_count=c)
    return y + b


def segsum(a):
    t = a.shape[-1]
    cs = jnp.cumsum(a, axis=-1)
    diff = cs[..., :, None] - cs[..., None, :]
    return jnp.where(jnp.tril(jnp.ones((t, t), dtype=bool)), diff, -jnp.inf)


def ssd_chunked(X, A, Bm, Cm):
    bsz, s, g, r, p = X.shape
    n = Bm.shape[-1]
    c, l = s // MB_CHUNK, MB_CHUNK
    Xc = X.reshape(bsz, c, l, g, r, p)
    Bc = Bm.reshape(bsz, c, l, g, n)
    Cc = Cm.reshape(bsz, c, l, g, n)
    Ac = A.reshape(bsz, c, l, g, r).transpose(0, 1, 3, 4, 2)
    Acum = jnp.cumsum(Ac, axis=-1)
    Lmat = jnp.exp(segsum(Ac))
    CB = jnp.einsum('bclgn,bcsgn->bcgls', Cc, Bc)
    y_diag = jnp.einsum('bcgrls,bcsgrp->bclgrp', CB[:, :, :, None] * Lmat, Xc)
    decay_states = jnp.exp(Acum[..., -1:] - Acum)
    states = jnp.einsum('bclgn,bcgrl,bclgrp->bcgrpn', Bc, decay_states, Xc)
    states = jnp.concatenate([jnp.zeros_like(states[:, :1]), states], axis=1)
    a_last = jnp.pad(Acum[..., -1], ((0, 0), (1, 0), (0, 0), (0, 0)))
    decay_chunk = jnp.exp(segsum(a_last.transpose(0, 2, 3, 1)))
    new_states = jnp.einsum('bgrzc,bcgrpn->bzgrpn', decay_chunk, states)
    states_in = new_states[:, :-1]
    y_off = jnp.einsum('bclgn,bcgrpn,bcgrl->bclgrp', Cc, states_in, jnp.exp(Acum))
    return (y_diag + y_off).reshape(bsz, s, g, r, p)


def mamba2_branch(z, xbc, dt, conv_w, conv_b, dt_bias, a_log, d_skip, norm_g):
    bsz, s = z.shape[0], z.shape[1]
    f32 = jnp.float32
    xbc = jax.nn.silu(causal_depthwise_conv(xbc, conv_w, conv_b))
    xs = xbc[..., :BRANCH]
    Bm = xbc[..., BRANCH:BRANCH + MB_GROUPS * MB_STATE]
    Cm = xbc[..., BRANCH + MB_GROUPS * MB_STATE:]
    dt = jax.nn.softplus(dt.astype(f32) + dt_bias.astype(f32))
    A = -jnp.exp(a_log.astype(f32)).reshape(MB_GROUPS, MB_PER_GROUP)
    dtg = dt.reshape(bsz, s, MB_GROUPS, MB_PER_GROUP)
    X = xs.astype(f32).reshape(bsz, s, MB_GROUPS, MB_PER_GROUP, MB_HEADDIM)
    y = ssd_chunked(X * dtg[..., None], dtg * A,
                    Bm.astype(f32).reshape(bsz, s, MB_GROUPS, MB_STATE),
                    Cm.astype(f32).reshape(bsz, s, MB_GROUPS, MB_STATE))
    y = y + d_skip.astype(f32).reshape(MB_GROUPS, MB_PER_GROUP)[:, :, None] * X
    y = y.reshape(bsz, s, BRANCH).astype(z.dtype) * jax.nn.silu(z)
    y = rmsnorm(y.reshape(bsz, s, MB_GROUPS, BRANCH // MB_GROUPS),
                norm_g.reshape(MB_GROUPS, BRANCH // MB_GROUPS))
    return y.reshape(bsz, s, BRANCH)


def complex_linear_combine(e1, e2):
    a1r, a1i, b1r, b1i = e1
    a2r, a2i, b2r, b2i = e2
    return (a2r * a1r - a2i * a1i, a2r * a1i + a2i * a1r,
            a2r * b1r - a2i * b1i + b2r, a2r * b1i + a2i * b1r + b2i)


def s5_branch(u, gate, lam_re, lam_im, log_step, b_re, b_im, c_re, c_im, d_skip, w_glu, b_glu):
    bsz, s = u.shape[0], u.shape[1]
    f32 = jnp.float32
    uf = u.astype(f32)
    ug = uf.reshape(bsz, s, S5_GROUPS, S5_GROUP)
    lr, li = lam_re.astype(f32), lam_im.astype(f32)
    step = jnp.exp(log_step.astype(f32))[:, None]
    mag = jnp.exp(lr * step)
    ab_re, ab_im = mag * jnp.cos(li * step), mag * jnp.sin(li * step)
    den = lr * lr + li * li
    f_re = ((ab_re - 1.0) * lr + ab_im * li) / den
    f_im = (ab_im * lr - (ab_re - 1.0) * li) / den
    br, bi = b_re.astype(f32), b_im.astype(f32)
    bb_re = f_re[..., None] * br - f_im[..., None] * bi
    bb_im = f_re[..., None] * bi + f_im[..., None] * br
    bu_re = jnp.einsum('bsgc,gpc->sbgp', ug, bb_re)
    bu_im = jnp.einsum('bsgc,gpc->sbgp', ug, bb_im)
    a_re = jnp.broadcast_to(ab_re, (s, 1) + ab_re.shape)
    a_im = jnp.broadcast_to(ab_im, (s, 1) + ab_im.shape)
    _, _, h_re, h_im = lax.associative_scan(complex_linear_combine, (a_re, a_im, bu_re, bu_im), axis=0)
    y = (jnp.einsum('sbgp,gcp->bsgc', h_re, c_re.astype(f32))
         - jnp.einsum('sbgp,gcp->bsgc', h_im, c_im.astype(f32)))
    y = y.reshape(bsz, s, BRANCH) + d_skip.astype(f32) * uf
    y = jax.nn.gelu(y)
    y = y * jax.nn.sigmoid(y @ w_glu.astype(f32) + b_glu.astype(f32))
    return y.astype(u.dtype) * jax.nn.silu(gate)


def mla_branch(c_q, c_kv, k_rope, gate, positions, q_a_norm, w_uq, kv_a_norm, w_ukv, q_norm, k_norm):
    bsz, s = c_q.shape[0], c_q.shape[1]
    q = (rmsnorm(c_q, q_a_norm) @ w_uq).reshape(bsz, s, MLA_HEADS, MLA_QK)
    kv = (rmsnorm(c_kv, kv_a_norm) @ w_ukv).reshape(bsz, s, MLA_HEADS, MLA_NOPE + MLA_V)
    k_nope, v = kv[..., :MLA_NOPE], kv[..., MLA_NOPE:]
    k = jnp.concatenate([k_nope, jnp.broadcast_to(k_rope[:, :, None, :], (bsz, s, MLA_HEADS, MLA_ROPE))], axis=-1)
    q = rmsnorm(q, q_norm)
    k = rmsnorm(k, k_norm)
    cos, sin = rope_tables(positions, MLA_ROPE, q.dtype)
    cos, sin = cos[:, :, None, :], sin[:, :, None, :]
    q = jnp.concatenate([q[..., :MLA_NOPE], rotate(q[..., MLA_NOPE:], cos, sin)], axis=-1)
    k = jnp.concatenate([k[..., :MLA_NOPE], rotate(k[..., MLA_NOPE:], cos, sin)], axis=-1)
    o = causal_block_attention(q[:, :, :, None], k[:, :, :, None], v, MLA_QK ** -0.5)
    return o.reshape(bsz, s, BRANCH) * jax.nn.silu(gate)


def setup_inputs(seed: int = 0) -> dict:
    key = jax.random.key(seed)
    ks = list(jax.random.split(key, 64))
    f32 = jnp.float32
    L = DEPTH

    def normal(shape, scale):
        return scale * jax.random.normal(ks.pop(), shape, f32)

    def gain(shape):
        return 1.0 + 0.02 * jax.random.normal(ks.pop(), shape, f32)

    def log_uniform(shape, lo, hi):
        return jax.random.uniform(ks.pop(), shape, f32, math.log(lo), math.log(hi))

    x = normal((BATCH, SEQ, D_MODEL), 1.0)
    positions = (jnp.arange(SEQ, dtype=jnp.int32)[None, :]
                 + jax.random.randint(ks.pop(), (BATCH, 1), 0, SEQ, dtype=jnp.int32))
    dt0 = jnp.exp(log_uniform((L, MB_HEADS), 1e-3, 1e-1))
    mb_dt_bias = dt0 + jnp.log(-jnp.expm1(-dt0))
    mb_a_log = jnp.log(jax.random.uniform(ks.pop(), (L, MB_HEADS), f32, 1.0, 16.0))
    s5_lam_re = -0.5 + normal((L, S5_GROUPS, S5_STATE), 0.01)
    s5_lam_im = (jnp.pi * jnp.arange(S5_STATE, dtype=f32))[None, None, :] + normal((L, S5_GROUPS, S5_STATE), 0.01)
    return {
        'x': x,
        'positions': positions,
        'norm_g': gain((L, D_MODEL)),
        'w_in': normal((L, D_MODEL, D_IN), D_MODEL ** -0.5),
        'da_q_norm': gain((L, DA_DHALF)),
        'da_k_norm': gain((L, DA_DHALF)),
        'da_lambda_q1': normal((L, DA_DHALF), 0.1),
        'da_lambda_k1': normal((L, DA_DHALF), 0.1),
        'da_lambda_q2': normal((L, DA_DHALF), 0.1),
        'da_lambda_k2': normal((L, DA_DHALF), 0.1),
        'da_subln': gain((L, DA_DV)),
        'mb_conv_w': normal((L, MB_CONV, MB_CONV_DIM), MB_CONV ** -0.5),
        'mb_conv_b': normal((L, MB_CONV_DIM), 0.02),
        'mb_dt_bias': mb_dt_bias,
        'mb_a_log': mb_a_log,
        'mb_d': gain((L, MB_HEADS)),
        'mb_norm': gain((L, BRANCH)),
        's5_lam_re': s5_lam_re,
        's5_lam_im': s5_lam_im,
        's5_log_step': log_uniform((L, S5_GROUPS), 1e-3, 1e-1),
        's5_b_re': normal((L, S5_GROUPS, S5_STATE, S5_GROUP), (2 * S5_GROUP) ** -0.5),
        's5_b_im': normal((L, S5_GROUPS, S5_STATE, S5_GROUP), (2 * S5_GROUP) ** -0.5),
        's5_c_re': normal((L, S5_GROUPS, S5_GROUP, S5_STATE), S5_STATE ** -0.5),
        's5_c_im': normal((L, S5_GROUPS, S5_GROUP, S5_STATE), S5_STATE ** -0.5),
        's5_d': gain((L, BRANCH)),
        's5_w_glu': normal((L, BRANCH, BRANCH), BRANCH ** -0.5),
        's5_b_glu': normal((L, BRANCH), 0.02),
        'mla_q_a_norm': gain((L, MLA_Q_RANK)),
        'mla_w_uq': normal((L, MLA_Q_RANK, MLA_HEADS * MLA_QK), MLA_Q_RANK ** -0.5),
        'mla_kv_a_norm': gain((L, MLA_KV_RANK)),
        'mla_w_ukv': normal((L, MLA_KV_RANK, MLA_HEADS * (MLA_NOPE + MLA_V)), MLA_KV_RANK ** -0.5),
        'mla_q_norm': gain((L, MLA_QK)),
        'mla_k_norm': gain((L, MLA_QK)),
        'w_br': normal((L, N_BRANCH, BRANCH, D_MODEL), BRANCH ** -0.5),
        'w_out': normal((L, D_MODEL, D_MODEL), D_MODEL ** -0.5),
    }


def reference(x, positions, norm_g, w_in, da_q_norm, da_k_norm, da_lambda_q1, da_lambda_k1,
              da_lambda_q2, da_lambda_k2, da_subln, mb_conv_w, mb_conv_b, mb_dt_bias, mb_a_log,
              mb_d, mb_norm, s5_lam_re, s5_lam_im, s5_log_step, s5_b_re, s5_b_im, s5_c_re,
              s5_c_im, s5_d, s5_w_glu, s5_b_glu, mla_q_a_norm, mla_w_uq, mla_kv_a_norm,
              mla_w_ukv, mla_q_norm, mla_k_norm, w_br, w_out):
    gate_names = ('gate_a', 'gate_b', 'gate_c', 'gate_d')
    for l in range(DEPTH):
        h = rmsnorm(x, norm_g[l])
        w = w_in[l]
        lambda_init = 0.8 - 0.6 * math.exp(-0.3 * l)
        y_a = diff_attention(in_proj(h, w, 'da_q'), in_proj(h, w, 'da_k'), in_proj(h, w, 'da_v'),
                             in_proj(h, w, 'da_gate'), positions, da_q_norm[l], da_k_norm[l],
                             da_lambda_q1[l], da_lambda_k1[l], da_lambda_q2[l], da_lambda_k2[l],
                             da_subln[l], lambda_init)
        y_b = mamba2_branch(in_proj(h, w, 'mb_z'), in_proj(h, w, 'mb_xbc'), in_proj(h, w, 'mb_dt'),
                            mb_conv_w[l], mb_conv_b[l], mb_dt_bias[l], mb_a_log[l], mb_d[l], mb_norm[l])
        y_c = s5_branch(in_proj(h, w, 's5_u'), in_proj(h, w, 's5_gate'), s5_lam_re[l], s5_lam_im[l],
                        s5_log_step[l], s5_b_re[l], s5_b_im[l], s5_c_re[l], s5_c_im[l], s5_d[l],
                        s5_w_glu[l], s5_b_glu[l])
        y_d = mla_branch(in_proj(h, w, 'mla_cq'), in_proj(h, w, 'mla_ckv'), in_proj(h, w, 'mla_krope'),
                         in_proj(h, w, 'mla_gate'), positions, mla_q_a_norm[l], mla_w_uq[l],
                         mla_kv_a_norm[l], mla_w_ukv[l], mla_q_norm[l], mla_k_norm[l])
        branches = (y_a, y_b, y_c, y_d)
        merged = jax.nn.sigmoid(in_proj(h, w, gate_names[0])) * (branches[0] @ w_br[l, 0])
        for n in range(1, N_BRANCH):
            merged = merged + jax.nn.sigmoid(in_proj(h, w, gate_names[n])) * (branches[n] @ w_br[l, n])
        x = x + merged @ w_out[l]
    return x
```

```python
import functools
import math

import jax
import jax.numpy as jnp
from jax import lax
from jax.experimental import pallas as pl
from jax.experimental.pallas import tpu as pltpu

F32 = jnp.float32
_ACT = jnp.bfloat16
_MXU = jnp.bfloat16

D_MODEL = 1024
BRANCH = 512
ROPE_THETA = 500000.0
EPS = 1e-6

DA_HEADS = 4
DA_DHALF = 64
DA_ROT = 16

MB_HEADS = 8
MB_STATE = 128
MB_CONV = 4
MB_CHUNK = 128

S5_GROUP = 16
S5_GROUPS = 32
S5_STATE = 64
S5_CHUNK = 16

MLA_HEADS = 4
MLA_Q_RANK = 256
MLA_KV_RANK = 128
MLA_NOPE = 128
MLA_ROPE = 64
MLA_QK = MLA_NOPE + MLA_ROPE

LANE = 128
ATT_BLOCK = 256
VMEM_LIMIT = 48 * 1024 * 1024

_SRC_LAYOUT = (
    ('da_q', 512), ('da_k', 512), ('da_v', 512), ('da_gate', 512), ('mb_z', 512),
    ('mb_xbc', 1024), ('mb_dt', 8), ('s5_u', 512), ('s5_gate', 512), ('mla_cq', 256),
    ('mla_ckv', 128), ('mla_krope', 64), ('mla_gate', 512),
    ('gate_a', 1024), ('gate_b', 1024), ('gate_c', 1024), ('gate_d', 1024),
)
_DST_LAYOUT = (
    ('gate_a', 1024), ('gate_b', 1024), ('gate_c', 1024), ('gate_d', 1024), ('mb_xbc', 1024),
    ('da_q', 512), ('da_k', 512), ('da_v', 512), ('da_gate', 512), ('mb_z', 512),
    ('s5_u', 512), ('s5_gate', 512), ('mla_gate', 512),
    ('mla_cq', 256), ('mla_ckv', 128), ('mla_krope', 64),
)


def _offsets(layout):
    out, start = {}, 0
    for name, n in layout:
        out[name] = start
        start += n
    return out, start


_SRC_OFF, _ = _offsets(_SRC_LAYOUT)
_COL, _NP_USED = _offsets(_DST_LAYOUT)
NP = 9728
assert NP - _NP_USED == 64 and NP % LANE == 0


def _params(n_axes):
    return pltpu.CompilerParams(dimension_semantics=("arbitrary",) * n_axes,
                                vmem_limit_bytes=VMEM_LIMIT)


def _silu(x):
    return x * jax.nn.sigmoid(x)


def _rms(x, gain, n):
    ss = jnp.sum(x * x, axis=-1, keepdims=True)
    return x * lax.rsqrt(ss * (1.0 / n) + EPS) * gain


def _inproj_kernel(x_ref, g_ref, w_ref, wdt_ref, p_ref, dt_ref, h_ref):
    @pl.when(pl.program_id(1) == 0)
    def _():
        h = _rms(x_ref[...], g_ref[...], D_MODEL).astype(_MXU)
        h_ref[...] = h
        dt_ref[...] = jnp.dot(h, wdt_ref[...], preferred_element_type=F32)

    p_ref[...] = jnp.dot(h_ref[...], w_ref[...], preferred_element_type=F32).astype(p_ref.dtype)


def _in_proj(x2d, norm_g, w_packed, w_dt):
    t = x2d.shape[0]
    tm = min(1024, t)
    tn = NP // 4
    return pl.pallas_call(
        _inproj_kernel,
        grid=(t // tm, NP // tn),
        in_specs=[
            pl.BlockSpec((tm, D_MODEL), lambda i, j: (i, 0)),
            pl.BlockSpec((1, D_MODEL), lambda i, j: (0, 0)),
            pl.BlockSpec((D_MODEL, tn), lambda i, j: (0, j)),
            pl.BlockSpec((D_MODEL, LANE), lambda i, j: (0, 0)),
        ],
        out_specs=[
            pl.BlockSpec((tm, tn), lambda i, j: (i, j)),
            pl.BlockSpec((tm, LANE), lambda i, j: (i, 0)),
        ],
        out_shape=[jax.ShapeDtypeStruct((t, NP), _ACT), jax.ShapeDtypeStruct((t, LANE), F32)],
        scratch_shapes=[pltpu.VMEM((tm, D_MODEL), _MXU)],
        compiler_params=_params(2),
        name="in_proj",
    )(x2d, norm_g.reshape(1, D_MODEL), w_packed, w_dt)


def _flash(qq, k_ref, v_ref, m_ref, l_ref, acc_ref, qi, tq):
    rows = qq.shape[0]
    m_ref[...] = jnp.full(m_ref.shape, -jnp.inf, F32)
    l_ref[...] = jnp.zeros(l_ref.shape, F32)
    acc_ref[...] = jnp.zeros(acc_ref.shape, F32)

    def step(j, masked):
        start = pl.multiple_of(j * tq, tq)
        k = k_ref[pl.ds(start, tq), :]
        v = v_ref[pl.ds(start, tq), :]
        s = lax.dot_general(qq, k, (((1,), (1,)), ((), ())), preferred_element_type=F32)
        if masked:
            row = lax.broadcasted_iota(jnp.int32, (rows, tq), 0) & (tq - 1)
            col = lax.broadcasted_iota(jnp.int32, (rows, tq), 1)
            s = jnp.where(col <= row, s, -jnp.inf)
        m_prev = m_ref[...]
        m_new = jnp.maximum(m_prev, jnp.max(s, axis=-1, keepdims=True))
        alpha = jnp.exp(m_prev - m_new)
        p = jnp.exp(s - jnp.concatenate([m_new] * (tq // LANE), axis=1))
        l_ref[...] = alpha * l_ref[...] + jnp.sum(p, axis=-1, keepdims=True)
        acc_ref[...] = alpha * acc_ref[...] + jnp.dot(p.astype(_MXU), v, preferred_element_type=F32)
        m_ref[...] = m_new

    def body(j, carry):
        step(j, False)
        return carry

    lax.fori_loop(0, qi, body, 0)
    step(qi, True)


def _da_kernel(q_ref, k_ref, v_ref, gate_ref, cq_ref, sq_ref, ck_ref, sk_ref, qg_ref, kg_ref,
               sub_ref, lam_ref, o_ref, kp_ref, m_ref, l_ref, acc_ref, *, tq, lambda_init):
    qi = pl.program_id(2)
    lane = lax.broadcasted_iota(jnp.int32, (1, LANE), 1)
    first = lane < DA_DHALF
    rot_lo = (lane & (DA_DHALF - 1)) < DA_ROT // 2

    def prep(x, gain, cos, sin):
        x2 = x * x
        s1 = jnp.sum(jnp.where(first, x2, 0.0), axis=-1, keepdims=True)
        s2 = jnp.sum(jnp.where(first, 0.0, x2), axis=-1, keepdims=True)
        r = jnp.where(first, lax.rsqrt(s1 * (1.0 / DA_DHALF) + EPS),
                      lax.rsqrt(s2 * (1.0 / DA_DHALF) + EPS))
        y = x * r * gain
        y_sw = jnp.where(rot_lo, pltpu.roll(y, LANE - DA_ROT // 2, 1), pltpu.roll(y, DA_ROT // 2, 1))
        return y * cos + y_sw * sin

    @pl.when(qi == 0)
    def _():
        kp_ref[...] = prep(k_ref[...].astype(F32), kg_ref[...], ck_ref[...], sk_ref[...]).astype(_MXU)

    q = prep(q_ref[...].astype(F32), qg_ref[...], cq_ref[...], sq_ref[...]) * (DA_DHALF ** -0.5)
    qq = jnp.concatenate([jnp.where(first, q, 0.0), jnp.where(first, 0.0, q)], axis=0).astype(_MXU)
    _flash(qq, kp_ref, v_ref, m_ref, l_ref, acc_ref, qi, tq)

    o = acc_ref[...] / l_ref[...]
    lv = lam_ref[...]
    lam = (jnp.exp(jnp.sum(lv[0:1] * lv[1:2], axis=-1, keepdims=True))
           - jnp.exp(jnp.sum(lv[2:3] * lv[3:4], axis=-1, keepdims=True)) + lambda_init)
    d = o[:tq] - lam * o[tq:]
    y = _rms(d, sub_ref[...], LANE) * (1.0 - lambda_init)
    o_ref[...] = (y * _silu(gate_ref[...].astype(F32))).astype(o_ref.dtype)


def _da_branch(p, tabs, qg, kg, lamv, subln, lambda_init, bsz, s):
    t = bsz * s
    tq = min(ATT_BLOCK, s)
    nq = s // tq
    cos, sin = tabs
    cq, ck, cv, cg = (_COL[n] // LANE for n in ('da_q', 'da_k', 'da_v', 'da_gate'))
    row = lambda b, h, i: b * nq + i
    kern = functools.partial(_da_kernel, tq=tq, lambda_init=lambda_init)
    return pl.pallas_call(
        kern,
        grid=(bsz, DA_HEADS, nq),
        in_specs=[
            pl.BlockSpec((tq, LANE), lambda b, h, i: (row(b, h, i), cq + h)),
            pl.BlockSpec((s, LANE), lambda b, h, i: (b, ck + h)),
            pl.BlockSpec((s, LANE), lambda b, h, i: (b, cv + h)),
            pl.BlockSpec((tq, LANE), lambda b, h, i: (row(b, h, i), cg + h)),
            pl.BlockSpec((tq, LANE), lambda b, h, i: (row(b, h, i), 0)),
            pl.BlockSpec((tq, LANE), lambda b, h, i: (row(b, h, i), 0)),
            pl.BlockSpec((s, LANE), lambda b, h, i: (b, 0)),
            pl.BlockSpec((s, LANE), lambda b, h, i: (b, 0)),
            pl.BlockSpec((1, LANE), lambda b, h, i: (0, 0)),
            pl.BlockSpec((1, LANE), lambda b, h, i: (0, 0)),
            pl.BlockSpec((1, LANE), lambda b, h, i: (0, 0)),
            pl.BlockSpec((4, DA_DHALF), lambda b, h, i: (0, 0)),
        ],
        out_specs=pl.BlockSpec((tq, LANE), lambda b, h, i: (row(b, h, i), h)),
        out_shape=jax.ShapeDtypeStruct((t, BRANCH), _ACT),
        scratch_shapes=[
            pltpu.VMEM((s, LANE), _MXU),
            pltpu.VMEM((2 * tq, LANE), F32),
            pltpu.VMEM((2 * tq, LANE), F32),
            pltpu.VMEM((2 * tq, LANE), F32),
        ],
        compiler_params=_params(3),
        name="diff_attention",
    )(p, p, p, p, cos, sin, cos, sin, qg, kg, subln, lamv)


def _mla_kernel(cq_ref, ckv_ref, kr_ref, gate_ref, cosq_ref, sinq_ref, cosk_ref, sink_ref,
                qan_ref, kvan_ref, wuq_ref, wukv_ref, qn_ref, kn_ref, o_ref,
                kp_ref, vp_ref, m_ref, l_ref, acc_ref, *, tq):
    qi = pl.program_id(2)
    lane = lax.broadcasted_iota(jnp.int32, (1, LANE), 1)
    rot_lo = lane < MLA_ROPE // 2

    def rope(x, cos, sin):
        x_sw = jnp.where(rot_lo, pltpu.roll(x, LANE - MLA_ROPE // 2, 1), pltpu.roll(x, MLA_ROPE // 2, 1))
        return x * cos + x_sw * sin

    @pl.when(qi == 0)
    def _():
        c = _rms(ckv_ref[...].astype(F32), kvan_ref[...], MLA_KV_RANK).astype(_MXU)
        kv = jnp.dot(c, wukv_ref[...], preferred_element_type=F32)
        k_nope = kv[:, :MLA_NOPE]
        k_rope = kr_ref[...].astype(F32)
        ss = (jnp.sum(k_nope * k_nope, axis=-1, keepdims=True)
              + jnp.sum(k_rope * k_rope, axis=-1, keepdims=True))
        r = lax.rsqrt(ss * (1.0 / MLA_QK) + EPS)
        gain = kn_ref[...]
        kp_ref[:, :MLA_NOPE] = (k_nope * r * gain[:, :MLA_NOPE]).astype(_MXU)
        kp_ref[:, MLA_NOPE:] = rope(k_rope * r * gain[:, MLA_NOPE:], cosk_ref[...], sink_ref[...]).astype(_MXU)
        vp_ref[...] = kv[:, MLA_NOPE:].astype(_MXU)

    c = _rms(cq_ref[...].astype(F32), qan_ref[...], MLA_Q_RANK).astype(_MXU)
    q = jnp.dot(c, wuq_ref[...], preferred_element_type=F32)
    q = _rms(q, qn_ref[...], MLA_QK) * (MLA_QK ** -0.5)
    qq = jnp.concatenate([q[:, :MLA_NOPE], rope(q[:, MLA_NOPE:], cosq_ref[...], sinq_ref[...])],
                         axis=1).astype(_MXU)
    _flash(qq, kp_ref, vp_ref, m_ref, l_ref, acc_ref, qi, tq)
    o = acc_ref[...] / l_ref[...]
    o_ref[...] = (o * _silu(gate_ref[...].astype(F32))).astype(o_ref.dtype)


def _mla_branch(p, tabs, qan, kvan, wuq, wukv, qn, kn, bsz, s):
    t = bsz * s
    tq = min(ATT_BLOCK, s)
    nq = s // tq
    cos, sin = tabs
    ccq = _COL['mla_cq'] // MLA_Q_RANK
    cckv, ckr, cg = (_COL[n] // LANE for n in ('mla_ckv', 'mla_krope', 'mla_gate'))
    row = lambda b, h, i: b * nq + i
    dk = 2 * LANE
    return pl.pallas_call(
        functools.partial(_mla_kernel, tq=tq),
        grid=(bsz, MLA_HEADS, nq),
        in_specs=[
            pl.BlockSpec((tq, MLA_Q_RANK), lambda b, h, i: (row(b, h, i), ccq)),
            pl.BlockSpec((s, LANE), lambda b, h, i: (b, cckv)),
            pl.BlockSpec((s, LANE), lambda b, h, i: (b, ckr)),
            pl.BlockSpec((tq, LANE), lambda b, h, i: (row(b, h, i), cg + h)),
            pl.BlockSpec((tq, LANE), lambda b, h, i: (row(b, h, i), 0)),
            pl.BlockSpec((tq, LANE), lambda b, h, i: (row(b, h, i), 0)),
            pl.BlockSpec((s, LANE), lambda b, h, i: (b, 0)),
            pl.BlockSpec((s, LANE), lambda b, h, i: (b, 0)),
            pl.BlockSpec((1, MLA_Q_RANK), lambda b, h, i: (0, 0)),
            pl.BlockSpec((1, MLA_KV_RANK), lambda b, h, i: (0, 0)),
            pl.BlockSpec((None, MLA_Q_RANK, dk), lambda b, h, i: (h, 0, 0)),
            pl.BlockSpec((None, MLA_KV_RANK, dk), lambda b, h, i: (h, 0, 0)),
            pl.BlockSpec((1, dk), lambda b, h, i: (0, 0)),
            pl.BlockSpec((1, dk), lambda b, h, i: (0, 0)),
        ],
        out_specs=pl.BlockSpec((tq, LANE), lambda b, h, i: (row(b, h, i), h)),
        out_shape=jax.ShapeDtypeStruct((t, BRANCH), _ACT),
        scratch_shapes=[
            pltpu.VMEM((s, dk), _MXU),
            pltpu.VMEM((s, LANE), _MXU),
            pltpu.VMEM((tq, LANE), F32),
            pltpu.VMEM((tq, LANE), F32),
            pltpu.VMEM((tq, LANE), F32),
        ],
        compiler_params=_params(3),
        name="mla_attention",
    )(p, p, p, p, cos, sin, cos, sin, qan, kvan, wuq, wukv, qn, kn)


def _mamba_kernel(z_ref, xbc_ref, dt_ref, cw_ref, cb_ref, dtb_ref, alog_ref, d_ref, ng_ref,
                  o_ref, buf_ref, st_ref, *, lc):
    halo = 8

    @pl.when(pl.program_id(1) == 0)
    def _():
        buf_ref[0:halo, :] = jnp.zeros((halo, buf_ref.shape[1]), F32)
        st_ref[...] = jnp.zeros(st_ref.shape, F32)

    buf_ref[halo:halo + lc, :] = xbc_ref[...].astype(F32)
    conv = cb_ref[...]
    for k in range(MB_CONV):
        conv = conv + cw_ref[k:k + 1, :] * buf_ref[pl.ds(halo - MB_CONV + 1 + k, lc), :]
    buf_ref[0:halo, :] = buf_ref[lc:lc + halo, :]
    xa = _silu(conv)
    xs = xa[:, :BRANCH]
    bm = xa[:, BRANCH:BRANCH + 2 * MB_STATE]
    cm = xa[:, BRANCH + 2 * MB_STATE:]

    dt = jax.nn.softplus(dt_ref[...] + dtb_ref[...])
    a = dt * (-jnp.exp(alog_ref[...]))
    r_i = lax.broadcasted_iota(jnp.int32, (lc, lc), 0)
    c_i = lax.broadcasted_iota(jnp.int32, (lc, lc), 1)
    tril = r_i >= c_i
    cs = jnp.dot(tril.astype(F32), a, preferred_element_type=F32,
                 precision=lax.Precision.HIGHEST)
    cs_t = cs.T
    cs_last = cs[lc - 1:lc, :]

    lane = lax.broadcasted_iota(jnp.int32, (1, LANE), 1)
    left = lane < 64
    top = lax.broadcasted_iota(jnp.int32, (LANE, 1), 0) < 64

    def pair(arr, h0):
        return jnp.where(left, arr[:, h0:h0 + 1], arr[:, h0 + 1:h0 + 2])

    for g in range(2):
        bg = bm[:, g * MB_STATE:(g + 1) * MB_STATE].astype(_MXU)
        cg = cm[:, g * MB_STATE:(g + 1) * MB_STATE].astype(_MXU)
        cb = lax.dot_general(cg, bg, (((1,), (1,)), ((), ())), preferred_element_type=F32)
        for pp in range(2):
            pr = 2 * g + pp
            h0 = 2 * pr
            xp = xs[:, pr * LANE:(pr + 1) * LANE]
            xdt = xp * pair(dt, h0)
            cse = pair(cs, h0)
            xdt_m = xdt.astype(_MXU)
            ys = []
            for h in (h0, h0 + 1):
                seg = cs[:, h:h + 1] - cs_t[h:h + 1, :]
                lm = jnp.exp(jnp.where(tril, seg, -jnp.inf))
                ys.append(jnp.dot((cb * lm).astype(_MXU), xdt_m, preferred_element_type=F32))
            y = jnp.where(left, ys[0], ys[1])
            st = st_ref[pr]
            y = y + jnp.exp(cse) * lax.dot_general(cg, st.astype(_MXU), (((1,), (1,)), ((), ())),
                                                   preferred_element_type=F32)
            w_t = (xdt * jnp.exp(pair(cs_last, h0) - cse)).T.astype(_MXU)
            new = jnp.dot(w_t, bg, preferred_element_type=F32)
            dec = jnp.exp(jnp.where(top, cs_last[:, h0:h0 + 1], cs_last[:, h0 + 1:h0 + 2]))
            st_ref[pr] = st * dec + new
            y = y + d_ref[:, pr * LANE:(pr + 1) * LANE] * xp
            y = y * _silu(z_ref[:, pr * LANE:(pr + 1) * LANE].astype(F32))
            buf_ref[halo:halo + lc, pr * LANE:(pr + 1) * LANE] = y

    for g in range(2):
        sl = slice(g * 256, (g + 1) * 256)
        yg = buf_ref[halo:halo + lc, sl]
        o_ref[:, sl] = _rms(yg, ng_ref[:, sl], 256).astype(o_ref.dtype)


def _mamba_branch(p, dt, conv_w, conv_b, dt_bias, a_log, d_exp, norm_g, bsz, s):
    t = bsz * s
    lc = min(MB_CHUNK, s)
    nc = s // lc
    cz = _COL['mb_z'] // BRANCH
    cx = _COL['mb_xbc'] // 1024
    row = lambda b, c: b * nc + c
    one = lambda b, c: (0, 0)
    return pl.pallas_call(
        functools.partial(_mamba_kernel, lc=lc),
        grid=(bsz, nc),
        in_specs=[
            pl.BlockSpec((lc, BRANCH), lambda b, c: (row(b, c), cz)),
            pl.BlockSpec((lc, 1024), lambda b, c: (row(b, c), cx)),
            pl.BlockSpec((lc, LANE), lambda b, c: (row(b, c), 0)),
            pl.BlockSpec((MB_CONV, 1024), one),
            pl.BlockSpec((1, 1024), one),
            pl.BlockSpec((1, LANE), one),
            pl.BlockSpec((1, LANE), one),
            pl.BlockSpec((1, BRANCH), one),
            pl.BlockSpec((1, BRANCH), one),
        ],
        out_specs=pl.BlockSpec((lc, BRANCH), lambda b, c: (row(b, c), 0)),
        out_shape=jax.ShapeDtypeStruct((t, BRANCH), _ACT),
        scratch_shapes=[
            pltpu.VMEM((lc + 8, 1024), F32),
            pltpu.VMEM((4, LANE, MB_STATE), F32),
        ],
        compiler_params=_params(2),
        name="mamba_ssd",
    )(p, p, dt, conv_w, conv_b, dt_bias, a_log, d_exp, norm_g)


def _s5_param_kernel(pre_ref, pim_ref, qre_ref, qim_ref, bre_ref, bim_ref, cre_ref, cim_ref,
                     ctre_ref, ctim_ref, k_ref, mre_ref, mim_ref, nre_ref, nim_ref):
    pre, pim = pre_ref[...], pim_ref[...]
    bre, bim = bre_ref[...], bim_ref[...]
    mre = pre * bre - pim * bim
    mim = pre * bim + pim * bre
    hi = lax.Precision.HIGHEST
    dims = (((1,), (1,)), ((), ()))
    k_ref[...] = (lax.dot_general(mre, cre_ref[...], dims, precision=hi, preferred_element_type=F32)
                  - lax.dot_general(mim, cim_ref[...], dims, precision=hi, preferred_element_type=F32))
    mre_ref[...] = mre
    mim_ref[...] = mim
    qre, qim = qre_ref[...], qim_ref[...]
    ctre, ctim = ctre_ref[...], ctim_ref[...]
    nre_ref[...] = qre * ctre - qim * ctim
    nim_ref[...] = -(qre * ctim + qim * ctre)


def _s5_kernel(u_ref, toep_ref, wst_ref, wout_ref, ar_ref, ai_ref, y_ref, s2_ref, hin_ref, *, nb, nc):
    u = u_ref[...]
    s2_ref[...] = jnp.dot(u, wst_ref[...], preferred_element_type=F32)
    ar = ar_ref[...]
    ai = ai_ref[...]

    def body(c, carry):
        h, hs = carry
        r0 = pl.multiple_of(c * nb, nb)
        hin_ref[pl.ds(r0, nb), :] = h
        s = s2_ref[pl.ds(r0, nb), :]
        return h * ar + hs * ai + s[:, :LANE], hs * ar - h * ai + s[:, LANE:]

    zero = jnp.zeros((nb, LANE), F32)
    lax.fori_loop(0, nc, body, (zero, zero))
    y = jnp.dot(u, toep_ref[...], preferred_element_type=F32)
    y = y + jnp.dot(hin_ref[...].astype(_MXU), wout_ref[...], preferred_element_type=F32)
    y_ref[...] = y.astype(y_ref.dtype)


def _s5_weights(lam_re, lam_im, log_step, b_re, b_im, c_re, c_im):
    g, pn, l, w = S5_GROUPS, S5_STATE, S5_CHUNK, S5_GROUP
    lr, li = lam_re.astype(F32), lam_im.astype(F32)
    step = jnp.exp(log_step.astype(F32))[:, None]
    mag = jnp.exp(lr * step)
    ab_re, ab_im = mag * jnp.cos(li * step), mag * jnp.sin(li * step)
    den = lr * lr + li * li
    f_re = ((ab_re - 1.0) * lr + ab_im * li) / den
    f_im = (ab_im * lr - (ab_re - 1.0) * li) / den
    br, bi = b_re.astype(F32), b_im.astype(F32)
    bb_re = f_re[..., None] * br - f_im[..., None] * bi
    bb_im = f_re[..., None] * bi + f_im[..., None] * br
    d = jnp.arange(l + 1, dtype=F32)[None, :, None]
    pmag = jnp.exp(d * (lr * step)[:, None, :])
    pw_re = pmag * jnp.cos(d * (li * step)[:, None, :])
    pw_im = pmag * jnp.sin(d * (li * step)[:, None, :])

    rep = lambda x: jnp.repeat(x, w, axis=1)
    til = lambda x: jnp.tile(x, (1, l, 1))
    bt_re, bt_im = jnp.swapaxes(bb_re, 1, 2), jnp.swapaxes(bb_im, 1, 2)
    cr, ci = c_re.astype(F32), c_im.astype(F32)
    rows = l * w
    blk = lambda n: pl.BlockSpec((None, rows, n), lambda i: (i, 0, 0))
    sm = pl.BlockSpec((None, w, pn), lambda i: (i, 0, 0))
    k, mre, mim, nre, nim = pl.pallas_call(
        _s5_param_kernel,
        grid=(g,),
        in_specs=[blk(pn)] * 6 + [sm, sm] + [blk(pn)] * 2,
        out_specs=[blk(w)] + [blk(pn)] * 4,
        out_shape=[jax.ShapeDtypeStruct((g, rows, w), F32)] + [jax.ShapeDtypeStruct((g, rows, pn), F32)] * 4,
        compiler_params=_params(1),
        name="s5_params",
    )(rep(pw_re[:, :l]), rep(pw_im[:, :l]), rep(pw_re[:, 1:]), rep(pw_im[:, 1:]),
      til(bt_re), til(bt_im), cr, ci, til(cr), til(ci))

    k4 = jnp.concatenate([k.reshape(g, l, w, w), jnp.zeros((g, 1, w, w), F32)], axis=1)
    lag = jnp.arange(l)[None, :] - jnp.arange(l)[:, None]
    toep = k4[:, jnp.where(lag >= 0, lag, l)]
    toep = toep.transpose(0, 1, 3, 2, 4).reshape(g, rows, rows)
    flip = lambda m: m.reshape(g, l, w, pn)[:, ::-1].reshape(g, rows, pn)
    wst = jnp.concatenate([flip(mre), flip(mim), flip(mim), flip(mre)], axis=-1)
    wout = jnp.swapaxes(jnp.concatenate([nre, nim], axis=-1), 1, 2)
    ar = jnp.concatenate([pw_re[:, l], pw_re[:, l]], axis=-1)[:, None, :]
    ai = jnp.concatenate([-pw_im[:, l], pw_im[:, l]], axis=-1)[:, None, :]
    return toep.astype(_MXU), wst.astype(_MXU), wout.astype(_MXU), ar, ai


def _s5_scan(p, weights, bsz, s):
    toep, wst, wout, ar, ai = weights
    g, l, w = S5_GROUPS, S5_CHUNK, S5_GROUP
    t = bsz * s
    nc = s // l
    rows, cols = nc * bsz, l * w
    u = lax.slice_in_dim(p, _COL['s5_u'], _COL['s5_u'] + BRANCH, axis=1)
    u = u.reshape(bsz, nc, l, g, w).transpose(3, 1, 0, 2, 4).reshape(g, rows, cols)
    per = lambda *shape: pl.BlockSpec((None,) + shape, lambda i: (i, 0, 0))
    y = pl.pallas_call(
        functools.partial(_s5_kernel, nb=bsz, nc=nc),
        grid=(g,),
        in_specs=[per(rows, cols), per(cols, cols), per(cols, 2 * LANE), per(LANE, cols),
                  per(1, LANE), per(1, LANE)],
        out_specs=per(rows, cols),
        out_shape=jax.ShapeDtypeStruct((g, rows, cols), _ACT),
        scratch_shapes=[pltpu.VMEM((rows, 2 * LANE), F32), pltpu.VMEM((rows, LANE), F32)],
        compiler_params=_params(1),
        name="s5_scan",
    )(u, toep, wst, wout, ar, ai)
    return y.reshape(g, nc, bsz, l, w).transpose(2, 1, 3, 0, 4).reshape(t, BRANCH)


def _merge_kernel(ya_ref, yb_ref, ys_ref, yd_ref, u_ref, sg_ref, gates_ref, x_ref,
                  sd_ref, wglu_ref, bglu_ref, wbr_ref, wout_ref, o_ref):
    y = ys_ref[...].astype(F32) + sd_ref[...] * u_ref[...].astype(F32)
    y = 0.5 * y * (1.0 + jnp.tanh(math.sqrt(2.0 / math.pi) * (y + 0.044715 * (y * y * y))))
    y = y * jax.nn.sigmoid(jnp.dot(y.astype(_MXU), wglu_ref[...], preferred_element_type=F32) + bglu_ref[...])
    yc = y * _silu(sg_ref[...].astype(F32))
    branches = (ya_ref[...].astype(_MXU), yb_ref[...].astype(_MXU), yc.astype(_MXU), yd_ref[...].astype(_MXU))
    merged = None
    for n, br in enumerate(branches):
        gate = jax.nn.sigmoid(gates_ref[:, n * D_MODEL:(n + 1) * D_MODEL].astype(F32))
        term = gate * jnp.dot(br, wbr_ref[n], preferred_element_type=F32)
        merged = term if merged is None else merged + term
    o_ref[...] = x_ref[...] + jnp.dot(merged.astype(_MXU), wout_ref[...], preferred_element_type=F32)


def _merge(ya, yb, ys, yd, p, x2d, s5_d, w_glu, b_glu, w_br, w_out):
    t = x2d.shape[0]
    tm = min(512, t)
    cu, csg = _COL['s5_u'] // BRANCH, _COL['s5_gate'] // BRANCH
    br = pl.BlockSpec((tm, BRANCH), lambda i: (i, 0))
    return pl.pallas_call(
        _merge_kernel,
        grid=(t // tm,),
        in_specs=[
            br, br, br, br,
            pl.BlockSpec((tm, BRANCH), lambda i: (i, cu)),
            pl.BlockSpec((tm, BRANCH), lambda i: (i, csg)),
            pl.BlockSpec((tm, 4 * D_MODEL), lambda i: (i, 0)),
            pl.BlockSpec((tm, D_MODEL), lambda i: (i, 0)),
            pl.BlockSpec((1, BRANCH), lambda i: (0, 0)),
            pl.BlockSpec((BRANCH, BRANCH), lambda i: (0, 0)),
            pl.BlockSpec((1, BRANCH), lambda i: (0, 0)),
            pl.BlockSpec((4, BRANCH, D_MODEL), lambda i: (0, 0, 0)),
            pl.BlockSpec((D_MODEL, D_MODEL), lambda i: (0, 0)),
        ],
        out_specs=pl.BlockSpec((tm, D_MODEL), lambda i: (i, 0)),
        out_shape=jax.ShapeDtypeStruct((t, D_MODEL), F32),
        compiler_params=_params(1),
        name="merge_out",
    )(ya, yb, ys, yd, p, p, p, x2d, s5_d, w_glu, b_glu, w_br, w_out)


def _pack_w_in(w):
    cols = [lax.slice_in_dim(w, _SRC_OFF[n], _SRC_OFF[n] + k, axis=1) for n, k in _DST_LAYOUT]
    cols.append(jnp.zeros((D_MODEL, NP - _NP_USED), w.dtype))
    w_dt = lax.slice_in_dim(w, _SRC_OFF['mb_dt'], _SRC_OFF['mb_dt'] + MB_HEADS, axis=1)
    w_dt = jnp.pad(w_dt, ((0, 0), (0, LANE - MB_HEADS)))
    return jnp.concatenate(cols, axis=1).astype(_MXU), w_dt.astype(_MXU)


def _rope_tables(positions, rot, width, period):
    half = rot // 2
    inv_freq = 1.0 / (ROPE_THETA ** (jnp.arange(0, rot, 2, dtype=F32) / rot))
    ang = positions.astype(F32).reshape(-1, 1) * inv_freq
    cos, sin = jnp.cos(ang), jnp.sin(ang)
    t = ang.shape[0]
    pad = period - rot
    cos_p = jnp.concatenate([cos, cos, jnp.ones((t, pad), F32)], axis=1)
    sin_p = jnp.concatenate([-sin, sin, jnp.zeros((t, pad), F32)], axis=1)
    reps = width // period
    cos_p, sin_p = jnp.tile(cos_p, (1, reps)), jnp.tile(sin_p, (1, reps))
    if width < LANE:
        cos_p = jnp.concatenate([cos_p, jnp.ones((t, LANE - width), F32)], axis=1)
        sin_p = jnp.concatenate([sin_p, jnp.zeros((t, LANE - width), F32)], axis=1)
    return cos_p, sin_p


def _row(v, width=None):
    v = v.astype(F32).reshape(1, -1)
    if width is not None and v.shape[1] < width:
        v = jnp.pad(v, ((0, 0), (0, width - v.shape[1])))
    return v


def kernel(x, positions, norm_g, w_in, da_q_norm, da_k_norm, da_lambda_q1, da_lambda_k1,
           da_lambda_q2, da_lambda_k2, da_subln, mb_conv_w, mb_conv_b, mb_dt_bias, mb_a_log,
           mb_d, mb_norm, s5_lam_re, s5_lam_im, s5_log_step, s5_b_re, s5_b_im, s5_c_re,
           s5_c_im, s5_d, s5_w_glu, s5_b_glu, mla_q_a_norm, mla_w_uq, mla_kv_a_norm,
           mla_w_ukv, mla_q_norm, mla_k_norm, w_br, w_out):
    bsz, s, _ = x.shape
    depth = w_in.shape[0]
    x2d = x.reshape(bsz * s, D_MODEL)
    da_tabs = _rope_tables(positions, DA_ROT, LANE, DA_DHALF)
    mla_tabs = _rope_tables(positions, MLA_ROPE, MLA_ROPE, MLA_ROPE)
    for l in range(depth):
        lambda_init = 0.8 - 0.6 * math.exp(-0.3 * l)
        w_packed, w_dt = _pack_w_in(w_in[l])
        p, dt = _in_proj(x2d, norm_g[l], w_packed, w_dt)

        lamv = jnp.stack([da_lambda_q1[l], da_lambda_k1[l], da_lambda_q2[l], da_lambda_k2[l]]).astype(F32)
        y_a = _da_branch(p, da_tabs, _row(jnp.tile(da_q_norm[l], 2)), _row(jnp.tile(da_k_norm[l], 2)),
                         lamv, _row(da_subln[l]), lambda_init, bsz, s)

        y_b = _mamba_branch(p, dt, mb_conv_w[l].astype(F32), _row(mb_conv_b[l]),
                            _row(mb_dt_bias[l], LANE), _row(mb_a_log[l], LANE),
                            _row(jnp.repeat(mb_d[l], BRANCH // MB_HEADS)), _row(mb_norm[l]), bsz, s)

        s5w = _s5_weights(s5_lam_re[l], s5_lam_im[l], s5_log_step[l], s5_b_re[l], s5_b_im[l],
                          s5_c_re[l], s5_c_im[l])
        y_s = _s5_scan(p, s5w, bsz, s)

        wuq = mla_w_uq[l].reshape(MLA_Q_RANK, MLA_HEADS, MLA_QK)
        wuq = jnp.pad(wuq, ((0, 0), (0, 0), (0, 2 * LANE - MLA_QK))).transpose(1, 0, 2).astype(_MXU)
        wukv = mla_w_ukv[l].reshape(MLA_KV_RANK, MLA_HEADS, 2 * LANE).transpose(1, 0, 2).astype(_MXU)
        y_d = _mla_branch(p, mla_tabs, _row(mla_q_a_norm[l]), _row(mla_kv_a_norm[l]), wuq, wukv,
                          _row(mla_q_norm[l], 2 * LANE), _row(mla_k_norm[l], 2 * LANE), bsz, s)

        x2d = _merge(y_a, y_b, y_s, y_d, p, x2d, _row(s5_d[l]), s5_w_glu[l].astype(_MXU),
                     _row(s5_b_glu[l]), w_br[l].astype(_MXU), w_out[l].astype(_MXU))
    return x2d.reshape(bsz, s, D_MODEL)
```

```python
import functools
import math

import jax
import jax.numpy as jnp
from jax import lax
from jax.experimental import pallas as pl
from jax.experimental.pallas import tpu as pltpu

F32 = jnp.float32
_ACT = jnp.bfloat16
_MXU = jnp.bfloat16

D_MODEL = 1024
BRANCH = 512
ROPE_THETA = 500000.0
EPS = 1e-6

DA_HEADS = 4
DA_DHALF = 64
DA_ROT = 16

MB_HEADS = 8
MB_STATE = 128
MB_CONV = 4
MB_CHUNK = 128

S5_GROUP = 16
S5_GROUPS = 32
S5_STATE = 64
S5_CHUNK = 128

MLA_HEADS = 4
MLA_Q_RANK = 256
MLA_KV_RANK = 128
MLA_NOPE = 128
MLA_ROPE = 64
MLA_QK = MLA_NOPE + MLA_ROPE

LANE = 128
ATT_BLOCK = 256
VMEM_LIMIT = 48 * 1024 * 1024

_SRC_LAYOUT = (
    ('da_q', 512), ('da_k', 512), ('da_v', 512), ('da_gate', 512), ('mb_z', 512),
    ('mb_xbc', 1024), ('mb_dt', 8), ('s5_u', 512), ('s5_gate', 512), ('mla_cq', 256),
    ('mla_ckv', 128), ('mla_krope', 64), ('mla_gate', 512),
    ('gate_a', 1024), ('gate_b', 1024), ('gate_c', 1024), ('gate_d', 1024),
)
_DST_LAYOUT = (
    ('gate_a', 1024), ('gate_b', 1024), ('gate_c', 1024), ('gate_d', 1024), ('mb_xbc', 1024),
    ('da_q', 512), ('da_k', 512), ('da_v', 512), ('da_gate', 512), ('mb_z', 512),
    ('s5_u', 512), ('s5_gate', 512), ('mla_gate', 512),
    ('mla_cq', 256), ('mla_ckv', 128), ('mla_krope', 64),
)


def _offsets(layout):
    out, start = {}, 0
    for name, n in layout:
        out[name] = start
        start += n
    return out, start


_SRC_OFF, _ = _offsets(_SRC_LAYOUT)
_COL, _NP_USED = _offsets(_DST_LAYOUT)
NP = 9728
assert NP - _NP_USED == 64 and NP % LANE == 0


def _params(n_axes):
    return pltpu.CompilerParams(dimension_semantics=("arbitrary",) * n_axes,
                                vmem_limit_bytes=VMEM_LIMIT)


def _silu(x):
    return x * jax.nn.sigmoid(x)


def _rms(x, gain, n):
    ss = jnp.sum(x * x, axis=-1, keepdims=True)
    return x * lax.rsqrt(ss * (1.0 / n) + EPS) * gain


def _inproj_kernel(x_ref, g_ref, w_ref, wdt_ref, p_ref, dt_ref, h_ref):
    @pl.when(pl.program_id(1) == 0)
    def _():
        h = _rms(x_ref[...], g_ref[...], D_MODEL).astype(_MXU)
        h_ref[...] = h
        dt_ref[...] = jnp.dot(h, wdt_ref[...], preferred_element_type=F32)

    p_ref[...] = jnp.dot(h_ref[...], w_ref[...], preferred_element_type=F32).astype(p_ref.dtype)


def _in_proj(x2d, norm_g, w_packed, w_dt):
    t = x2d.shape[0]
    tm = min(1024, t)
    tn = NP // 4
    return pl.pallas_call(
        _inproj_kernel,
        grid=(t // tm, NP // tn),
        in_specs=[
            pl.BlockSpec((tm, D_MODEL), lambda i, j: (i, 0)),
            pl.BlockSpec((1, D_MODEL), lambda i, j: (0, 0)),
            pl.BlockSpec((D_MODEL, tn), lambda i, j: (0, j)),
            pl.BlockSpec((D_MODEL, LANE), lambda i, j: (0, 0)),
        ],
        out_specs=[
            pl.BlockSpec((tm, tn), lambda i, j: (i, j)),
            pl.BlockSpec((tm, LANE), lambda i, j: (i, 0)),
        ],
        out_shape=[jax.ShapeDtypeStruct((t, NP), _ACT), jax.ShapeDtypeStruct((t, LANE), F32)],
        scratch_shapes=[pltpu.VMEM((tm, D_MODEL), _MXU)],
        compiler_params=_params(2),
        name="in_proj",
    )(x2d, norm_g.reshape(1, D_MODEL), w_packed, w_dt)


def _flash(qq_ref, k_ref, v_ref, m_ref, l_ref, acc_ref, qi, tq, dk):
    heads, rows, _ = qq_ref.shape
    m_ref[...] = jnp.full(m_ref.shape, -jnp.inf, F32)
    l_ref[...] = jnp.zeros(l_ref.shape, F32)
    acc_ref[...] = jnp.zeros(acc_ref.shape, F32)

    def step(j, masked):
        start = pl.multiple_of(j * tq, tq)
        if masked:
            row = lax.broadcasted_iota(jnp.int32, (rows, tq), 0) & (tq - 1)
            col = lax.broadcasted_iota(jnp.int32, (rows, tq), 1)
            keep = col <= row
        for h in range(heads):
            k = k_ref[pl.ds(start, tq), h * dk:(h + 1) * dk]
            v = v_ref[pl.ds(start, tq), h * LANE:(h + 1) * LANE]
            s = lax.dot_general(qq_ref[h], k, (((1,), (1,)), ((), ())), preferred_element_type=F32)
            if masked:
                s = jnp.where(keep, s, -jnp.inf)
            m_prev = m_ref[h]
            m_new = jnp.maximum(m_prev, jnp.max(s, axis=-1, keepdims=True))
            alpha = jnp.exp(m_prev - m_new)
            p = jnp.exp(s - jnp.concatenate([m_new] * (tq // LANE), axis=1))
            l_ref[h] = alpha * l_ref[h] + jnp.sum(p, axis=-1, keepdims=True)
            acc_ref[h] = alpha * acc_ref[h] + jnp.dot(p.astype(_MXU), v, preferred_element_type=F32)
            m_ref[h] = m_new

    def body(j, carry):
        step(j, False)
        return carry

    lax.fori_loop(0, qi, body, 0)
    step(qi, True)


def _head(h, width=LANE):
    return slice(h * width, (h + 1) * width)


def _da_kernel(q_ref, k_ref, v_ref, gate_ref, cq_ref, sq_ref, ck_ref, sk_ref, qg_ref, kg_ref,
               sub_ref, lam_ref, o_ref, kp_ref, qq_ref, m_ref, l_ref, acc_ref, *, tq, lambda_init):
    qi = pl.program_id(1)
    lane = lax.broadcasted_iota(jnp.int32, (1, LANE), 1)
    first = lane < DA_DHALF
    rot_lo = (lane & (DA_DHALF - 1)) < DA_ROT // 2

    def prep(x, gain, cos, sin):
        x2 = x * x
        s1 = jnp.sum(jnp.where(first, x2, 0.0), axis=-1, keepdims=True)
        s2 = jnp.sum(jnp.where(first, 0.0, x2), axis=-1, keepdims=True)
        r = jnp.where(first, lax.rsqrt(s1 * (1.0 / DA_DHALF) + EPS),
                      lax.rsqrt(s2 * (1.0 / DA_DHALF) + EPS))
        y = x * r * gain
        y_sw = jnp.where(rot_lo, pltpu.roll(y, LANE - DA_ROT // 2, 1), pltpu.roll(y, DA_ROT // 2, 1))
        return y * cos + y_sw * sin

    @pl.when(qi == 0)
    def _():
        for h in range(DA_HEADS):
            kp_ref[:, _head(h)] = prep(k_ref[:, _head(h)].astype(F32), kg_ref[...], ck_ref[...],
                                       sk_ref[...]).astype(_MXU)

    for h in range(DA_HEADS):
        q = prep(q_ref[:, _head(h)].astype(F32), qg_ref[...], cq_ref[...], sq_ref[...]) * (DA_DHALF ** -0.5)
        qq_ref[h] = jnp.concatenate([jnp.where(first, q, 0.0), jnp.where(first, 0.0, q)], axis=0).astype(_MXU)
    _flash(qq_ref, kp_ref, v_ref, m_ref, l_ref, acc_ref, qi, tq, LANE)

    lv = lam_ref[...]
    lam = (jnp.exp(jnp.sum(lv[0:1] * lv[1:2], axis=-1, keepdims=True))
           - jnp.exp(jnp.sum(lv[2:3] * lv[3:4], axis=-1, keepdims=True)) + lambda_init)
    for h in range(DA_HEADS):
        o = acc_ref[h] / l_ref[h]
        d = o[:tq] - lam * o[tq:]
        y = _rms(d, sub_ref[...], LANE) * (1.0 - lambda_init)
        o_ref[:, _head(h)] = (y * _silu(gate_ref[:, _head(h)].astype(F32))).astype(o_ref.dtype)


def _da_branch(p, tabs, qg, kg, lamv, subln, lambda_init, bsz, s):
    t = bsz * s
    tq = min(ATT_BLOCK, s)
    nq = s // tq
    cos, sin = tabs
    cq, ck, cv, cg = (_COL[n] // BRANCH for n in ('da_q', 'da_k', 'da_v', 'da_gate'))
    row = lambda b, i: b * nq + i
    one = lambda b, i: (0, 0)
    kern = functools.partial(_da_kernel, tq=tq, lambda_init=lambda_init)
    return pl.pallas_call(
        kern,
        grid=(bsz, nq),
        in_specs=[
            pl.BlockSpec((tq, BRANCH), lambda b, i: (row(b, i), cq)),
            pl.BlockSpec((s, BRANCH), lambda b, i: (b, ck)),
            pl.BlockSpec((s, BRANCH), lambda b, i: (b, cv)),
            pl.BlockSpec((tq, BRANCH), lambda b, i: (row(b, i), cg)),
            pl.BlockSpec((tq, LANE), lambda b, i: (row(b, i), 0)),
            pl.BlockSpec((tq, LANE), lambda b, i: (row(b, i), 0)),
            pl.BlockSpec((s, LANE), lambda b, i: (b, 0)),
            pl.BlockSpec((s, LANE), lambda b, i: (b, 0)),
            pl.BlockSpec((1, LANE), one),
            pl.BlockSpec((1, LANE), one),
            pl.BlockSpec((1, LANE), one),
            pl.BlockSpec((4, DA_DHALF), one),
        ],
        out_specs=pl.BlockSpec((tq, BRANCH), lambda b, i: (row(b, i), 0)),
        out_shape=jax.ShapeDtypeStruct((t, BRANCH), _ACT),
        scratch_shapes=[
            pltpu.VMEM((s, BRANCH), _MXU),
            pltpu.VMEM((DA_HEADS, 2 * tq, LANE), _MXU),
            pltpu.VMEM((DA_HEADS, 2 * tq, LANE), F32),
            pltpu.VMEM((DA_HEADS, 2 * tq, LANE), F32),
            pltpu.VMEM((DA_HEADS, 2 * tq, LANE), F32),
        ],
        compiler_params=_params(2),
        name="diff_attention",
    )(p, p, p, p, cos, sin, cos, sin, qg, kg, subln, lamv)


def _mla_kernel(cq_ref, ckv_ref, kr_ref, gate_ref, cosq_ref, sinq_ref, cosk_ref, sink_ref,
                qan_ref, kvan_ref, wuq_ref, wukv_ref, qn_ref, kn_ref, o_ref,
                kp_ref, vp_ref, qq_ref, m_ref, l_ref, acc_ref, *, tq):
    qi = pl.program_id(1)
    dk = 2 * LANE
    lane = lax.broadcasted_iota(jnp.int32, (1, LANE), 1)
    rot_lo = lane < MLA_ROPE // 2

    def rope(x, cos, sin):
        x_sw = jnp.where(rot_lo, pltpu.roll(x, LANE - MLA_ROPE // 2, 1), pltpu.roll(x, MLA_ROPE // 2, 1))
        return x * cos + x_sw * sin

    @pl.when(qi == 0)
    def _():
        c = _rms(ckv_ref[...].astype(F32), kvan_ref[...], MLA_KV_RANK).astype(_MXU)
        k_rope = kr_ref[...].astype(F32)
        ss_rope = jnp.sum(k_rope * k_rope, axis=-1, keepdims=True)
        gain = kn_ref[...]
        for h in range(MLA_HEADS):
            kv = jnp.dot(c, wukv_ref[:, _head(h, dk)], preferred_element_type=F32)
            k_nope = kv[:, :MLA_NOPE]
            ss = jnp.sum(k_nope * k_nope, axis=-1, keepdims=True) + ss_rope
            r = lax.rsqrt(ss * (1.0 / MLA_QK) + EPS)
            kp_ref[:, h * dk:h * dk + MLA_NOPE] = (k_nope * r * gain[:, :MLA_NOPE]).astype(_MXU)
            kp_ref[:, h * dk + MLA_NOPE:(h + 1) * dk] = rope(
                k_rope * r * gain[:, MLA_NOPE:], cosk_ref[...], sink_ref[...]).astype(_MXU)
            vp_ref[:, _head(h)] = kv[:, MLA_NOPE:].astype(_MXU)

    c = _rms(cq_ref[...].astype(F32), qan_ref[...], MLA_Q_RANK).astype(_MXU)
    for h in range(MLA_HEADS):
        q = jnp.dot(c, wuq_ref[:, _head(h, dk)], preferred_element_type=F32)
        q = _rms(q, qn_ref[...], MLA_QK) * (MLA_QK ** -0.5)
        qq_ref[h] = jnp.concatenate([q[:, :MLA_NOPE], rope(q[:, MLA_NOPE:], cosq_ref[...], sinq_ref[...])],
                                    axis=1).astype(_MXU)
    _flash(qq_ref, kp_ref, vp_ref, m_ref, l_ref, acc_ref, qi, tq, dk)
    for h in range(MLA_HEADS):
        o = acc_ref[h] / l_ref[h]
        o_ref[:, _head(h)] = (o * _silu(gate_ref[:, _head(h)].astype(F32))).astype(o_ref.dtype)


def _mla_branch(p, tabs, qan, kvan, wuq, wukv, qn, kn, bsz, s):
    t = bsz * s
    tq = min(ATT_BLOCK, s)
    nq = s // tq
    cos, sin = tabs
    ccq = _COL['mla_cq'] // MLA_Q_RANK
    cckv, ckr = (_COL[n] // LANE for n in ('mla_ckv', 'mla_krope'))
    cg = _COL['mla_gate'] // BRANCH
    row = lambda b, i: b * nq + i
    one = lambda b, i: (0, 0)
    dk = 2 * LANE
    return pl.pallas_call(
        functools.partial(_mla_kernel, tq=tq),
        grid=(bsz, nq),
        in_specs=[
            pl.BlockSpec((tq, MLA_Q_RANK), lambda b, i: (row(b, i), ccq)),
            pl.BlockSpec((s, LANE), lambda b, i: (b, cckv)),
            pl.BlockSpec((s, LANE), lambda b, i: (b, ckr)),
            pl.BlockSpec((tq, BRANCH), lambda b, i: (row(b, i), cg)),
            pl.BlockSpec((tq, LANE), lambda b, i: (row(b, i), 0)),
            pl.BlockSpec((tq, LANE), lambda b, i: (row(b, i), 0)),
            pl.BlockSpec((s, LANE), lambda b, i: (b, 0)),
            pl.BlockSpec((s, LANE), lambda b, i: (b, 0)),
            pl.BlockSpec((1, MLA_Q_RANK), one),
            pl.BlockSpec((1, MLA_KV_RANK), one),
            pl.BlockSpec((MLA_Q_RANK, MLA_HEADS * dk), one),
            pl.BlockSpec((MLA_KV_RANK, MLA_HEADS * dk), one),
            pl.BlockSpec((1, dk), one),
            pl.BlockSpec((1, dk), one),
        ],
        out_specs=pl.BlockSpec((tq, BRANCH), lambda b, i: (row(b, i), 0)),
        out_shape=jax.ShapeDtypeStruct((t, BRANCH), _ACT),
        scratch_shapes=[
            pltpu.VMEM((s, MLA_HEADS * dk), _MXU),
            pltpu.VMEM((s, BRANCH), _MXU),
            pltpu.VMEM((MLA_HEADS, tq, dk), _MXU),
            pltpu.VMEM((MLA_HEADS, tq, LANE), F32),
            pltpu.VMEM((MLA_HEADS, tq, LANE), F32),
            pltpu.VMEM((MLA_HEADS, tq, LANE), F32),
        ],
        compiler_params=_params(2),
        name="mla_attention",
    )(p, p, p, p, cos, sin, cos, sin, qan, kvan, wuq, wukv, qn, kn)


def _mamba_kernel(z_ref, xbc_ref, dt_ref, cw_ref, cb_ref, dtb_ref, alog_ref, d_ref, ng_ref,
                  o_ref, buf_ref, st_ref, *, lc):
    halo = 8

    @pl.when(pl.program_id(1) == 0)
    def _():
        buf_ref[0:halo, :] = jnp.zeros((halo, buf_ref.shape[1]), F32)
        st_ref[...] = jnp.zeros(st_ref.shape, F32)

    buf_ref[halo:halo + lc, :] = xbc_ref[...].astype(F32)
    conv = cb_ref[...]
    for k in range(MB_CONV):
        conv = conv + cw_ref[k:k + 1, :] * buf_ref[pl.ds(halo - MB_CONV + 1 + k, lc), :]
    buf_ref[0:halo, :] = buf_ref[lc:lc + halo, :]
    xa = _silu(conv)
    xs = xa[:, :BRANCH]
    bm = xa[:, BRANCH:BRANCH + 2 * MB_STATE]
    cm = xa[:, BRANCH + 2 * MB_STATE:]

    dt = jax.nn.softplus(dt_ref[...] + dtb_ref[...])
    a = dt * (-jnp.exp(alog_ref[...]))
    r_i = lax.broadcasted_iota(jnp.int32, (lc, lc), 0)
    c_i = lax.broadcasted_iota(jnp.int32, (lc, lc), 1)
    tril = r_i >= c_i
    cs = jnp.dot(tril.astype(F32), a, preferred_element_type=F32,
                 precision=lax.Precision.HIGHEST)
    cs_t = cs.T
    cs_last = cs[lc - 1:lc, :]

    lane = lax.broadcasted_iota(jnp.int32, (1, LANE), 1)
    left = lane < 64
    top = lax.broadcasted_iota(jnp.int32, (LANE, 1), 0) < 64

    def pair(arr, h0):
        return jnp.where(left, arr[:, h0:h0 + 1], arr[:, h0 + 1:h0 + 2])

    for g in range(2):
        bg = bm[:, g * MB_STATE:(g + 1) * MB_STATE].astype(_MXU)
        cg = cm[:, g * MB_STATE:(g + 1) * MB_STATE].astype(_MXU)
        cb = lax.dot_general(cg, bg, (((1,), (1,)), ((), ())), preferred_element_type=F32)
        for pp in range(2):
            pr = 2 * g + pp
            h0 = 2 * pr
            xp = xs[:, pr * LANE:(pr + 1) * LANE]
            xdt = xp * pair(dt, h0)
            cse = pair(cs, h0)
            xdt_m = xdt.astype(_MXU)
            ys = []
            for h in (h0, h0 + 1):
                seg = cs[:, h:h + 1] - cs_t[h:h + 1, :]
                lm = jnp.exp(jnp.where(tril, seg, -jnp.inf))
                ys.append(jnp.dot((cb * lm).astype(_MXU), xdt_m, preferred_element_type=F32))
            y = jnp.where(left, ys[0], ys[1])
            st = st_ref[pr]
            y = y + jnp.exp(cse) * lax.dot_general(cg, st.astype(_MXU), (((1,), (1,)), ((), ())),
                                                   preferred_element_type=F32)
            w_t = (xdt * jnp.exp(pair(cs_last, h0) - cse)).T.astype(_MXU)
            new = jnp.dot(w_t, bg, preferred_element_type=F32)
            dec = jnp.exp(jnp.where(top, cs_last[:, h0:h0 + 1], cs_last[:, h0 + 1:h0 + 2]))
            st_ref[pr] = st * dec + new
            y = y + d_ref[:, pr * LANE:(pr + 1) * LANE] * xp
            y = y * _silu(z_ref[:, pr * LANE:(pr + 1) * LANE].astype(F32))
            buf_ref[halo:halo + lc, pr * LANE:(pr + 1) * LANE] = y

    for g in range(2):
        sl = slice(g * 256, (g + 1) * 256)
        yg = buf_ref[halo:halo + lc, sl]
        o_ref[:, sl] = _rms(yg, ng_ref[:, sl], 256).astype(o_ref.dtype)


def _mamba_branch(p, dt, conv_w, conv_b, dt_bias, a_log, d_exp, norm_g, bsz, s):
    t = bsz * s
    lc = min(MB_CHUNK, s)
    nc = s // lc
    cz = _COL['mb_z'] // BRANCH
    cx = _COL['mb_xbc'] // 1024
    row = lambda b, c: b * nc + c
    one = lambda b, c: (0, 0)
    return pl.pallas_call(
        functools.partial(_mamba_kernel, lc=lc),
        grid=(bsz, nc),
        in_specs=[
            pl.BlockSpec((lc, BRANCH), lambda b, c: (row(b, c), cz)),
            pl.BlockSpec((lc, 1024), lambda b, c: (row(b, c), cx)),
            pl.BlockSpec((lc, LANE), lambda b, c: (row(b, c), 0)),
            pl.BlockSpec((MB_CONV, 1024), one),
            pl.BlockSpec((1, 1024), one),
            pl.BlockSpec((1, LANE), one),
            pl.BlockSpec((1, LANE), one),
            pl.BlockSpec((1, BRANCH), one),
            pl.BlockSpec((1, BRANCH), one),
        ],
        out_specs=pl.BlockSpec((lc, BRANCH), lambda b, c: (row(b, c), 0)),
        out_shape=jax.ShapeDtypeStruct((t, BRANCH), _ACT),
        scratch_shapes=[
            pltpu.VMEM((lc + 8, 1024), F32),
            pltpu.VMEM((4, LANE, MB_STATE), F32),
        ],
        compiler_params=_params(2),
        name="mamba_ssd",
    )(p, p, dt, conv_w, conv_b, dt_bias, a_log, d_exp, norm_g)


def _s5_param_kernel(pre_ref, pim_ref, qre_ref, qim_ref, rre_ref, rim_ref, bre_ref, bim_ref,
                     cre_ref, cim_ref, kvec_ref, wst_ref, wout_ref):
    l, w = S5_CHUNK, S5_GROUP
    hi = lax.Precision.HIGHEST
    nt = (((1,), (1,)), ((), ()))
    p_re, p_im = pre_ref[...], pim_ref[...]
    q_re, q_im = qre_ref[...], qim_ref[...]
    r_re, r_im = rre_ref[...], rim_ref[...]
    c_re, c_im = cre_ref[...], cim_ref[...]
    for ci in range(w):
        b_re, b_im = bre_ref[ci:ci + 1, :], bim_ref[ci:ci + 1, :]
        m_re = p_re * b_re - p_im * b_im
        m_im = p_re * b_im + p_im * b_re
        kvec_ref[ci * w:(ci + 1) * w, :] = (
            lax.dot_general(c_re, m_re, nt, precision=hi, preferred_element_type=F32)
            - lax.dot_general(c_im, m_im, nt, precision=hi, preferred_element_type=F32))
        s_re = r_re * b_re - r_im * b_im
        s_im = r_re * b_im + r_im * b_re
        wst_ref[ci * l:(ci + 1) * l, :] = jnp.concatenate([s_re, s_im, s_im, s_re], axis=1).astype(_MXU)
    for co in range(w):
        g_re, g_im = c_re[co:co + 1, :], c_im[co:co + 1, :]
        n_re = q_re * g_re - q_im * g_im
        n_im = -(q_re * g_im + q_im * g_re)
        wout_ref[:, co * l:(co + 1) * l] = jnp.concatenate([n_re, n_im], axis=1).T.astype(_MXU)


def _s5_kernel(u_ref, kvec_ref, wst_ref, wout_ref, ar_ref, ai_ref, y_ref,
               toep_ref, u2_ref, sa_ref, sb_ref, hin_ref, *, nb, nc):
    l, w = S5_CHUNK, S5_GROUP
    keep = lax.broadcasted_iota(jnp.int32, (l, l), 1) >= lax.broadcasted_iota(jnp.int32, (l, l), 0)

    def build(ci, carry):
        r0 = pl.multiple_of(ci * l, l)
        for co in range(w):
            kv = jnp.broadcast_to(kvec_ref[pl.ds(ci * w + co, 1), :], (l, l))
            blk = pltpu.roll(kv, 0, 1, stride=1, stride_axis=0)
            toep_ref[pl.ds(r0, l), co * l:(co + 1) * l] = jnp.where(keep, blk, 0.0).astype(_MXU)
        return carry

    lax.fori_loop(0, w, build, 0)
    for ci in range(w):
        u2_ref[:, ci * l:(ci + 1) * l] = u_ref[ci]
    u = u2_ref[...]
    s2 = jnp.dot(u, wst_ref[...], preferred_element_type=F32)
    sa_ref[...] = s2[:, :LANE]
    sb_ref[...] = s2[:, LANE:]
    ar = ar_ref[...]
    ai = ai_ref[...]
    h = jnp.zeros((nb, LANE), F32)
    hs = h
    for c in range(nc):
        rows_c = pl.ds(c, nb, stride=nc)
        hin_ref[rows_c, :] = h
        h, hs = h * ar + hs * ai + sa_ref[rows_c, :], hs * ar - h * ai + sb_ref[rows_c, :]
    y = jnp.dot(u, toep_ref[...], preferred_element_type=F32)
    y = y + jnp.dot(hin_ref[...].astype(_MXU), wout_ref[...], preferred_element_type=F32)
    for co in range(w):
        y_ref[co] = y[:, co * l:(co + 1) * l].astype(y_ref.dtype)


def _s5_weights(lam_re, lam_im, log_step, b_re, b_im, c_re, c_im):
    g, pn, l, w = S5_GROUPS, S5_STATE, S5_CHUNK, S5_GROUP
    lr, li = lam_re.astype(F32), lam_im.astype(F32)
    step = jnp.exp(log_step.astype(F32))[:, None]
    mag = jnp.exp(lr * step)
    ab_re, ab_im = mag * jnp.cos(li * step), mag * jnp.sin(li * step)
    den = lr * lr + li * li
    f_re = ((ab_re - 1.0) * lr + ab_im * li) / den
    f_im = (ab_im * lr - (ab_re - 1.0) * li) / den
    br, bi = b_re.astype(F32), b_im.astype(F32)
    bb_re = f_re[..., None] * br - f_im[..., None] * bi
    bb_im = f_re[..., None] * bi + f_im[..., None] * br
    d = jnp.arange(l + 1, dtype=F32)[None, :, None]
    pmag = jnp.exp(d * (lr * step)[:, None, :])
    pw_re = pmag * jnp.cos(d * (li * step)[:, None, :])
    pw_im = pmag * jnp.sin(d * (li * step)[:, None, :])
    bt_re, bt_im = jnp.swapaxes(bb_re, 1, 2), jnp.swapaxes(bb_im, 1, 2)

    per = lambda *shape: pl.BlockSpec((None,) + shape, lambda i: (i, 0, 0))
    kvec, wst, wout = pl.pallas_call(
        _s5_param_kernel,
        grid=(g,),
        in_specs=[per(l, pn)] * 6 + [per(w, pn)] * 4,
        out_specs=[per(w * w, l), per(w * l, 2 * LANE), per(LANE, w * l)],
        out_shape=[jax.ShapeDtypeStruct((g, w * w, l), F32),
                   jax.ShapeDtypeStruct((g, w * l, 2 * LANE), _MXU),
                   jax.ShapeDtypeStruct((g, LANE, w * l), _MXU)],
        compiler_params=_params(1),
        name="s5_params",
    )(pw_re[:, :l], pw_im[:, :l], pw_re[:, 1:], pw_im[:, 1:], pw_re[:, l - 1::-1], pw_im[:, l - 1::-1],
      bt_re, bt_im, c_re.astype(F32), c_im.astype(F32))
    ar = jnp.concatenate([pw_re[:, l], pw_re[:, l]], axis=-1)[:, None, :]
    ai = jnp.concatenate([-pw_im[:, l], pw_im[:, l]], axis=-1)[:, None, :]
    return kvec, wst, wout, ar, ai


def _s5_scan(p, weights, bsz, s):
    kvec, wst, wout, ar, ai = weights
    g, l, w = S5_GROUPS, S5_CHUNK, S5_GROUP
    t = bsz * s
    nc = s // l
    rows = bsz * nc
    u_t = lax.slice_in_dim(p, _COL['s5_u'], _COL['s5_u'] + BRANCH, axis=1).T.reshape(BRANCH, rows, l)
    per = lambda *shape: pl.BlockSpec((None,) + shape, lambda i: (i, 0, 0))
    chan = pl.BlockSpec((w, rows, l), lambda i: (i, 0, 0))
    y_t = pl.pallas_call(
        functools.partial(_s5_kernel, nb=bsz, nc=nc),
        grid=(g,),
        in_specs=[chan, per(w * w, l), per(w * l, 2 * LANE), per(LANE, w * l), per(1, LANE), per(1, LANE)],
        out_specs=chan,
        out_shape=jax.ShapeDtypeStruct((BRANCH, rows, l), _ACT),
        scratch_shapes=[
            pltpu.VMEM((w * l, w * l), _MXU),
            pltpu.VMEM((rows, w * l), _MXU),
            pltpu.VMEM((rows, LANE), F32),
            pltpu.VMEM((rows, LANE), F32),
            pltpu.VMEM((rows, LANE), F32),
        ],
        compiler_params=_params(1),
        name="s5_scan",
    )(u_t, kvec, wst, wout, ar, ai)
    return y_t.reshape(BRANCH, t).T


def _merge_kernel(ya_ref, yb_ref, ys_ref, yd_ref, u_ref, sg_ref, gates_ref, x_ref,
                  sd_ref, wglu_ref, bglu_ref, wbr_ref, wout_ref, o_ref):
    y = ys_ref[...].astype(F32) + sd_ref[...] * u_ref[...].astype(F32)
    y = 0.5 * y * (1.0 + jnp.tanh(math.sqrt(2.0 / math.pi) * (y + 0.044715 * (y * y * y))))
    y = y * jax.nn.sigmoid(jnp.dot(y.astype(_MXU), wglu_ref[...], preferred_element_type=F32) + bglu_ref[...])
    yc = y * _silu(sg_ref[...].astype(F32))
    branches = (ya_ref[...].astype(_MXU), yb_ref[...].astype(_MXU), yc.astype(_MXU), yd_ref[...].astype(_MXU))
    merged = None
    for n, br in enumerate(branches):
        gate = jax.nn.sigmoid(gates_ref[:, n * D_MODEL:(n + 1) * D_MODEL].astype(F32))
        term = gate * jnp.dot(br, wbr_ref[n], preferred_element_type=F32)
        merged = term if merged is None else merged + term
    o_ref[...] = x_ref[...] + jnp.dot(merged.astype(_MXU), wout_ref[...], preferred_element_type=F32)


def _merge(ya, yb, ys, yd, p, x2d, s5_d, w_glu, b_glu, w_br, w_out):
    t = x2d.shape[0]
    tm = min(512, t)
    cu, csg = _COL['s5_u'] // BRANCH, _COL['s5_gate'] // BRANCH
    br = pl.BlockSpec((tm, BRANCH), lambda i: (i, 0))
    return pl.pallas_call(
        _merge_kernel,
        grid=(t // tm,),
        in_specs=[
            br, br, br, br,
            pl.BlockSpec((tm, BRANCH), lambda i: (i, cu)),
            pl.BlockSpec((tm, BRANCH), lambda i: (i, csg)),
            pl.BlockSpec((tm, 4 * D_MODEL), lambda i: (i, 0)),
            pl.BlockSpec((tm, D_MODEL), lambda i: (i, 0)),
            pl.BlockSpec((1, BRANCH), lambda i: (0, 0)),
            pl.BlockSpec((BRANCH, BRANCH), lambda i: (0, 0)),
            pl.BlockSpec((1, BRANCH), lambda i: (0, 0)),
            pl.BlockSpec((4, BRANCH, D_MODEL), lambda i: (0, 0, 0)),
            pl.BlockSpec((D_MODEL, D_MODEL), lambda i: (0, 0)),
        ],
        out_specs=pl.BlockSpec((tm, D_MODEL), lambda i: (i, 0)),
        out_shape=jax.ShapeDtypeStruct((t, D_MODEL), F32),
        compiler_params=_params(1),
        name="merge_out",
    )(ya, yb, ys, yd, p, p, p, x2d, s5_d, w_glu, b_glu, w_br, w_out)


def _pack_w_in(w):
    cols = [lax.slice_in_dim(w, _SRC_OFF[n], _SRC_OFF[n] + k, axis=1) for n, k in _DST_LAYOUT]
    cols.append(jnp.zeros((D_MODEL, NP - _NP_USED), w.dtype))
    w_dt = lax.slice_in_dim(w, _SRC_OFF['mb_dt'], _SRC_OFF['mb_dt'] + MB_HEADS, axis=1)
    w_dt = jnp.pad(w_dt, ((0, 0), (0, LANE - MB_HEADS)))
    return jnp.concatenate(cols, axis=1).astype(_MXU), w_dt.astype(_MXU)


def _rope_tables(positions, rot, width, period):
    half = rot // 2
    inv_freq = 1.0 / (ROPE_THETA ** (jnp.arange(0, rot, 2, dtype=F32) / rot))
    ang = positions.astype(F32).reshape(-1, 1) * inv_freq
    cos, sin = jnp.cos(ang), jnp.sin(ang)
    t = ang.shape[0]
    pad = period - rot
    cos_p = jnp.concatenate([cos, cos, jnp.ones((t, pad), F32)], axis=1)
    sin_p = jnp.concatenate([-sin, sin, jnp.zeros((t, pad), F32)], axis=1)
    reps = width // period
    cos_p, sin_p = jnp.tile(cos_p, (1, reps)), jnp.tile(sin_p, (1, reps))
    if width < LANE:
        cos_p = jnp.concatenate([cos_p, jnp.ones((t, LANE - width), F32)], axis=1)
        sin_p = jnp.concatenate([sin_p, jnp.zeros((t, LANE - width), F32)], axis=1)
    return cos_p, sin_p


def _row(v, width=None):
    v = v.astype(F32).reshape(1, -1)
    if width is not None and v.shape[1] < width:
        v = jnp.pad(v, ((0, 0), (0, width - v.shape[1])))
    return v


def kernel(x, positions, norm_g, w_in, da_q_norm, da_k_norm, da_lambda_q1, da_lambda_k1,
           da_lambda_q2, da_lambda_k2, da_subln, mb_conv_w, mb_conv_b, mb_dt_bias, mb_a_log,
           mb_d, mb_norm, s5_lam_re, s5_lam_im, s5_log_step, s5_b_re, s5_b_im, s5_c_re,
           s5_c_im, s5_d, s5_w_glu, s5_b_glu, mla_q_a_norm, mla_w_uq, mla_kv_a_norm,
           mla_w_ukv, mla_q_norm, mla_k_norm, w_br, w_out):
    bsz, s, _ = x.shape
    depth = w_in.shape[0]
    x2d = x.reshape(bsz * s, D_MODEL)
    da_tabs = _rope_tables(positions, DA_ROT, LANE, DA_DHALF)
    mla_tabs = _rope_tables(positions, MLA_ROPE, MLA_ROPE, MLA_ROPE)
    for l in range(depth):
        lambda_init = 0.8 - 0.6 * math.exp(-0.3 * l)
        w_packed, w_dt = _pack_w_in(w_in[l])
        p, dt = _in_proj(x2d, norm_g[l], w_packed, w_dt)

        lamv = jnp.stack([da_lambda_q1[l], da_lambda_k1[l], da_lambda_q2[l], da_lambda_k2[l]]).astype(F32)
        y_a = _da_branch(p, da_tabs, _row(jnp.tile(da_q_norm[l], 2)), _row(jnp.tile(da_k_norm[l], 2)),
                         lamv, _row(da_subln[l]), lambda_init, bsz, s)

        y_b = _mamba_branch(p, dt, mb_conv_w[l].astype(F32), _row(mb_conv_b[l]),
                            _row(mb_dt_bias[l], LANE), _row(mb_a_log[l], LANE),
                            _row(jnp.repeat(mb_d[l], BRANCH // MB_HEADS)), _row(mb_norm[l]), bsz, s)

        s5w = _s5_weights(s5_lam_re[l], s5_lam_im[l], s5_log_step[l], s5_b_re[l], s5_b_im[l],
                          s5_c_re[l], s5_c_im[l])
        y_s = _s5_scan(p, s5w, bsz, s)

        wuq = mla_w_uq[l].reshape(MLA_Q_RANK, MLA_HEADS, MLA_QK)
        wuq = jnp.pad(wuq, ((0, 0), (0, 0), (0, 2 * LANE - MLA_QK))).reshape(MLA_Q_RANK, -1).astype(_MXU)
        wukv = mla_w_ukv[l].astype(_MXU)
        y_d = _mla_branch(p, mla_tabs, _row(mla_q_a_norm[l]), _row(mla_kv_a_norm[l]), wuq, wukv,
                          _row(mla_q_norm[l], 2 * LANE), _row(mla_k_norm[l], 2 * LANE), bsz, s)

        x2d = _merge(y_a, y_b, y_s, y_d, p, x2d, _row(s5_d[l]), s5_w_glu[l].astype(_MXU),
                     _row(s5_b_glu[l]), w_br[l].astype(_MXU), w_out[l].astype(_MXU))
    return x2d.reshape(bsz, s, D_MODEL)
```

```python
import functools
import math

import jax
import jax.numpy as jnp
from jax import lax
from jax.experimental import pallas as pl
from jax.experimental.pallas import tpu as pltpu

F32 = jnp.float32
_ACT = jnp.bfloat16
_MXU = jnp.bfloat16

D_MODEL = 1024
BRANCH = 512
ROPE_THETA = 500000.0
EPS = 1e-6
LOG2E = math.log2(math.e)

DA_HEADS = 4
DA_DHALF = 64
DA_ROT = 16

MB_HEADS = 8
MB_STATE = 128
MB_CONV = 4
MB_CHUNK = 128

S5_GROUP = 16
S5_GROUPS = 32
S5_STATE = 64
S5_CHUNK = 128

MLA_HEADS = 4
MLA_Q_RANK = 256
MLA_KV_RANK = 128
MLA_NOPE = 128
MLA_ROPE = 64
MLA_QK = MLA_NOPE + MLA_ROPE

LANE = 128
ATT_BLOCK = 256
DA_TQ = 256
MLA_TQ = 512
VMEM_LIMIT = 48 * 1024 * 1024

_SRC_LAYOUT = (
    ('da_q', 512), ('da_k', 512), ('da_v', 512), ('da_gate', 512), ('mb_z', 512),
    ('mb_xbc', 1024), ('mb_dt', 8), ('s5_u', 512), ('s5_gate', 512), ('mla_cq', 256),
    ('mla_ckv', 128), ('mla_krope', 64), ('mla_gate', 512),
    ('gate_a', 1024), ('gate_b', 1024), ('gate_c', 1024), ('gate_d', 1024),
)
_DST_LAYOUT = (
    ('gate_a', 1024), ('gate_b', 1024), ('gate_c', 1024), ('gate_d', 1024), ('mb_xbc', 1024),
    ('da_q', 512), ('da_k', 512), ('da_v', 512), ('da_gate', 512), ('mb_z', 512),
    ('s5_u', 512), ('s5_gate', 512), ('mla_gate', 512),
    ('mla_cq', 256), ('mla_ckv', 128), ('mla_krope', 64),
)


def _offsets(layout):
    out, start = {}, 0
    for name, n in layout:
        out[name] = start
        start += n
    return out, start


_SRC_OFF, _ = _offsets(_SRC_LAYOUT)
_COL, _NP_USED = _offsets(_DST_LAYOUT)
NP = 9728
assert NP - _NP_USED == 64 and NP % LANE == 0


def _params(n_axes):
    return pltpu.CompilerParams(dimension_semantics=("arbitrary",) * n_axes,
                                vmem_limit_bytes=VMEM_LIMIT)


def _silu(x):
    return x * jax.nn.sigmoid(x)


def _rms(x, gain, n):
    ss = jnp.sum(x * x, axis=-1, keepdims=True)
    return x * lax.rsqrt(ss * (1.0 / n) + EPS) * gain


def _inproj_kernel(x_ref, g_ref, w_ref, wdt_ref, p_ref, dt_ref, h_ref):
    @pl.when(pl.program_id(1) == 0)
    def _():
        h = _rms(x_ref[...], g_ref[...], D_MODEL).astype(_MXU)
        h_ref[...] = h
        dt_ref[...] = jnp.dot(h, wdt_ref[...], preferred_element_type=F32)

    p_ref[...] = jnp.dot(h_ref[...], w_ref[...], preferred_element_type=F32).astype(p_ref.dtype)


def _in_proj(x2d, norm_g, w_packed, w_dt):
    t = x2d.shape[0]
    tm = min(1024, t)
    tn = NP // 4
    return pl.pallas_call(
        _inproj_kernel,
        grid=(t // tm, NP // tn),
        in_specs=[
            pl.BlockSpec((tm, D_MODEL), lambda i, j: (i, 0)),
            pl.BlockSpec((1, D_MODEL), lambda i, j: (0, 0)),
            pl.BlockSpec((D_MODEL, tn), lambda i, j: (0, j)),
            pl.BlockSpec((D_MODEL, LANE), lambda i, j: (0, 0)),
        ],
        out_specs=[
            pl.BlockSpec((tm, tn), lambda i, j: (i, j)),
            pl.BlockSpec((tm, LANE), lambda i, j: (i, 0)),
        ],
        out_shape=[jax.ShapeDtypeStruct((t, NP), _ACT), jax.ShapeDtypeStruct((t, LANE), F32)],
        scratch_shapes=[pltpu.VMEM((tm, D_MODEL), _MXU)],
        compiler_params=_params(2),
        name="in_proj",
    )(x2d, norm_g.reshape(1, D_MODEL), w_packed, w_dt)


def _flash(qq_ref, k_ref, v_ref, m_ref, acc_ref, qi, tq, dk):
    heads, rows, _ = qq_ref.shape
    tk = ATT_BLOCK
    m_ref[...] = jnp.full(m_ref.shape, -jnp.inf, F32)
    acc_ref[...] = jnp.zeros(acc_ref.shape, F32)

    def step(j, diag):
        start = pl.multiple_of(j * tk, tk)
        if diag is not None:
            row = lax.broadcasted_iota(jnp.int32, (rows, tk), 0) & (tq - 1)
            col = lax.broadcasted_iota(jnp.int32, (rows, tk), 1) + diag * tk
            keep = col <= row
        for h in range(heads):
            k = k_ref[pl.ds(start, tk), h * dk:(h + 1) * dk]
            v = v_ref[pl.ds(start, tk), h * 2 * LANE:(h + 1) * 2 * LANE]
            s = lax.dot_general(qq_ref[h], k, (((1,), (1,)), ((), ())), preferred_element_type=F32)
            if diag is not None:
                s = jnp.where(keep, s, -jnp.inf)
            m_prev = m_ref[h]
            m_new = jnp.maximum(m_prev, jnp.max(s, axis=-1, keepdims=True))
            alpha = jnp.exp2(m_prev - m_new)
            p = jnp.exp2(s - jnp.concatenate([m_new] * (tk // LANE), axis=1))
            acc_ref[h] = (jnp.concatenate([alpha, alpha], axis=1) * acc_ref[h]
                          + jnp.dot(p.astype(_MXU), v, preferred_element_type=F32))
            m_ref[h] = m_new

    def body(j, carry):
        step(j, None)
        return carry

    per_q = tq // tk
    lax.fori_loop(0, qi * per_q, body, 0)
    for d in range(per_q):
        step(qi * per_q + d, d)


def _head(h, width=LANE):
    return slice(h * width, (h + 1) * width)


def _da_kernel(q_ref, k_ref, v_ref, gate_ref, cq_ref, sq_ref, ck_ref, sk_ref, qg_ref, kg_ref,
               sub_ref, lam_ref, o_ref, kp_ref, vx_ref, qq_ref, m_ref, acc_ref, *, tq, lambda_init):
    qi = pl.program_id(1)
    lane = lax.broadcasted_iota(jnp.int32, (1, LANE), 1)
    first = lane < DA_DHALF
    rot_lo = (lane & (DA_DHALF - 1)) < DA_ROT // 2

    def prep(x, gain, cos, sin):
        x2 = x * x
        s1 = jnp.sum(jnp.where(first, x2, 0.0), axis=-1, keepdims=True)
        s2 = jnp.sum(jnp.where(first, 0.0, x2), axis=-1, keepdims=True)
        r = jnp.where(first, lax.rsqrt(s1 * (1.0 / DA_DHALF) + EPS),
                      lax.rsqrt(s2 * (1.0 / DA_DHALF) + EPS))
        y = x * r * gain
        y_sw = jnp.where(rot_lo, pltpu.roll(y, LANE - DA_ROT // 2, 1), pltpu.roll(y, DA_ROT // 2, 1))
        return y * cos + y_sw * sin

    @pl.when(qi == 0)
    def _():
        for h in range(DA_HEADS):
            kp_ref[:, _head(h)] = prep(k_ref[:, _head(h)].astype(F32), kg_ref[...], ck_ref[...],
                                       sk_ref[...]).astype(_MXU)
            vx_ref[:, 2 * h * LANE:(2 * h + 1) * LANE] = v_ref[:, _head(h)]
            vx_ref[:, (2 * h + 1) * LANE:(2 * h + 2) * LANE] = jnp.ones((vx_ref.shape[0], LANE), _MXU)

    for h in range(DA_HEADS):
        q = prep(q_ref[:, _head(h)].astype(F32), qg_ref[...], cq_ref[...], sq_ref[...]) * (DA_DHALF ** -0.5 * LOG2E)
        qq_ref[h] = jnp.concatenate([jnp.where(first, q, 0.0), jnp.where(first, 0.0, q)], axis=0).astype(_MXU)
    _flash(qq_ref, kp_ref, vx_ref, m_ref, acc_ref, qi, tq, LANE)

    lv = lam_ref[...]
    lam = (jnp.exp(jnp.sum(lv[0:1] * lv[1:2], axis=-1, keepdims=True))
           - jnp.exp(jnp.sum(lv[2:3] * lv[3:4], axis=-1, keepdims=True)) + lambda_init)
    for h in range(DA_HEADS):
        o = acc_ref[h, :, :LANE] / acc_ref[h, :, LANE:]
        d = o[:tq] - lam * o[tq:]
        y = _rms(d, sub_ref[...], LANE) * (1.0 - lambda_init)
        o_ref[:, _head(h)] = (y * _silu(gate_ref[:, _head(h)].astype(F32))).astype(o_ref.dtype)


def _da_branch(p, tabs, qg, kg, lamv, subln, lambda_init, bsz, s):
    t = bsz * s
    tq = min(DA_TQ, s)
    nq = s // tq
    cos, sin = tabs
    cq, ck, cv, cg = (_COL[n] // BRANCH for n in ('da_q', 'da_k', 'da_v', 'da_gate'))
    row = lambda b, i: b * nq + i
    one = lambda b, i: (0, 0)
    kern = functools.partial(_da_kernel, tq=tq, lambda_init=lambda_init)
    return pl.pallas_call(
        kern,
        grid=(bsz, nq),
        in_specs=[
            pl.BlockSpec((tq, BRANCH), lambda b, i: (row(b, i), cq)),
            pl.BlockSpec((s, BRANCH), lambda b, i: (b, ck)),
            pl.BlockSpec((s, BRANCH), lambda b, i: (b, cv)),
            pl.BlockSpec((tq, BRANCH), lambda b, i: (row(b, i), cg)),
            pl.BlockSpec((tq, LANE), lambda b, i: (row(b, i), 0)),
            pl.BlockSpec((tq, LANE), lambda b, i: (row(b, i), 0)),
            pl.BlockSpec((s, LANE), lambda b, i: (b, 0)),
            pl.BlockSpec((s, LANE), lambda b, i: (b, 0)),
            pl.BlockSpec((1, LANE), one),
            pl.BlockSpec((1, LANE), one),
            pl.BlockSpec((1, LANE), one),
            pl.BlockSpec((4, DA_DHALF), one),
        ],
        out_specs=pl.BlockSpec((tq, BRANCH), lambda b, i: (row(b, i), 0)),
        out_shape=jax.ShapeDtypeStruct((t, BRANCH), _ACT),
        scratch_shapes=[
            pltpu.VMEM((s, BRANCH), _MXU),
            pltpu.VMEM((s, DA_HEADS * 2 * LANE), _MXU),
            pltpu.VMEM((DA_HEADS, 2 * tq, LANE), _MXU),
            pltpu.VMEM((DA_HEADS, 2 * tq, LANE), F32),
            pltpu.VMEM((DA_HEADS, 2 * tq, 2 * LANE), F32),
        ],
        compiler_params=_params(2),
        name="diff_attention",
    )(p, p, p, p, cos, sin, cos, sin, qg, kg, subln, lamv)


def _mla_kernel(cq_ref, ckv_ref, kr_ref, gate_ref, cosq_ref, sinq_ref, cosk_ref, sink_ref,
                qan_ref, kvan_ref, wuq_ref, wukv_ref, qn_ref, kn_ref, o_ref,
                kp_ref, vp_ref, qq_ref, m_ref, acc_ref, *, tq):
    qi = pl.program_id(1)
    dk = 2 * LANE
    lane = lax.broadcasted_iota(jnp.int32, (1, LANE), 1)
    rot_lo = lane < MLA_ROPE // 2

    def rope(x, cos, sin):
        x_sw = jnp.where(rot_lo, pltpu.roll(x, LANE - MLA_ROPE // 2, 1), pltpu.roll(x, MLA_ROPE // 2, 1))
        return x * cos + x_sw * sin

    @pl.when(qi == 0)
    def _():
        c = _rms(ckv_ref[...].astype(F32), kvan_ref[...], MLA_KV_RANK).astype(_MXU)
        k_rope = kr_ref[...].astype(F32)
        ss_rope = jnp.sum(k_rope * k_rope, axis=-1, keepdims=True)
        gain = kn_ref[...]
        for h in range(MLA_HEADS):
            kv = jnp.dot(c, wukv_ref[:, _head(h, dk)], preferred_element_type=F32)
            k_nope = kv[:, :MLA_NOPE]
            ss = jnp.sum(k_nope * k_nope, axis=-1, keepdims=True) + ss_rope
            r = lax.rsqrt(ss * (1.0 / MLA_QK) + EPS)
            kp_ref[:, h * dk:h * dk + MLA_NOPE] = (k_nope * r * gain[:, :MLA_NOPE]).astype(_MXU)
            kp_ref[:, h * dk + MLA_NOPE:(h + 1) * dk] = rope(
                k_rope * r * gain[:, MLA_NOPE:], cosk_ref[...], sink_ref[...]).astype(_MXU)
            vp_ref[:, 2 * h * LANE:(2 * h + 1) * LANE] = kv[:, MLA_NOPE:].astype(_MXU)
            vp_ref[:, (2 * h + 1) * LANE:(2 * h + 2) * LANE] = jnp.ones((vp_ref.shape[0], LANE), _MXU)

    c = _rms(cq_ref[...].astype(F32), qan_ref[...], MLA_Q_RANK).astype(_MXU)
    for h in range(MLA_HEADS):
        q = jnp.dot(c, wuq_ref[:, _head(h, dk)], preferred_element_type=F32)
        q = _rms(q, qn_ref[...], MLA_QK) * (MLA_QK ** -0.5 * LOG2E)
        qq_ref[h] = jnp.concatenate([q[:, :MLA_NOPE], rope(q[:, MLA_NOPE:], cosq_ref[...], sinq_ref[...])],
                                    axis=1).astype(_MXU)
    _flash(qq_ref, kp_ref, vp_ref, m_ref, acc_ref, qi, tq, dk)
    for h in range(MLA_HEADS):
        o = acc_ref[h, :, :LANE] / acc_ref[h, :, LANE:]
        o_ref[:, _head(h)] = (o * _silu(gate_ref[:, _head(h)].astype(F32))).astype(o_ref.dtype)


def _mla_branch(p, tabs, qan, kvan, wuq, wukv, qn, kn, bsz, s):
    t = bsz * s
    tq = min(MLA_TQ, s)
    nq = s // tq
    cos, sin = tabs
    ccq = _COL['mla_cq'] // MLA_Q_RANK
    cckv, ckr = (_COL[n] // LANE for n in ('mla_ckv', 'mla_krope'))
    cg = _COL['mla_gate'] // BRANCH
    row = lambda b, i: b * nq + i
    one = lambda b, i: (0, 0)
    dk = 2 * LANE
    return pl.pallas_call(
        functools.partial(_mla_kernel, tq=tq),
        grid=(bsz, nq),
        in_specs=[
            pl.BlockSpec((tq, MLA_Q_RANK), lambda b, i: (row(b, i), ccq)),
            pl.BlockSpec((s, LANE), lambda b, i: (b, cckv)),
            pl.BlockSpec((s, LANE), lambda b, i: (b, ckr)),
            pl.BlockSpec((tq, BRANCH), lambda b, i: (row(b, i), cg)),
            pl.BlockSpec((tq, LANE), lambda b, i: (row(b, i), 0)),
            pl.BlockSpec((tq, LANE), lambda b, i: (row(b, i), 0)),
            pl.BlockSpec((s, LANE), lambda b, i: (b, 0)),
            pl.BlockSpec((s, LANE), lambda b, i: (b, 0)),
            pl.BlockSpec((1, MLA_Q_RANK), one),
            pl.BlockSpec((1, MLA_KV_RANK), one),
            pl.BlockSpec((MLA_Q_RANK, MLA_HEADS * dk), one),
            pl.BlockSpec((MLA_KV_RANK, MLA_HEADS * dk), one),
            pl.BlockSpec((1, dk), one),
            pl.BlockSpec((1, dk), one),
        ],
        out_specs=pl.BlockSpec((tq, BRANCH), lambda b, i: (row(b, i), 0)),
        out_shape=jax.ShapeDtypeStruct((t, BRANCH), _ACT),
        scratch_shapes=[
            pltpu.VMEM((s, MLA_HEADS * dk), _MXU),
            pltpu.VMEM((s, MLA_HEADS * 2 * LANE), _MXU),
            pltpu.VMEM((MLA_HEADS, tq, dk), _MXU),
            pltpu.VMEM((MLA_HEADS, tq, LANE), F32),
            pltpu.VMEM((MLA_HEADS, tq, 2 * LANE), F32),
        ],
        compiler_params=_params(2),
        name="mla_attention",
    )(p, p, p, p, cos, sin, cos, sin, qan, kvan, wuq, wukv, qn, kn)


def _mamba_kernel(z_ref, xbc_ref, dt_ref, cw_ref, cb_ref, dtb_ref, alog_ref, d_ref, ng_ref,
                  o_ref, xb_ref, y_ref, st_ref, *, lc):
    halo = 8

    @pl.when(pl.program_id(1) == 0)
    def _():
        xb_ref[0:halo, :] = jnp.zeros((halo, xb_ref.shape[1]), F32)
        st_ref[...] = jnp.zeros(st_ref.shape, F32)

    xb_ref[halo:halo + lc, :] = xbc_ref[...].astype(F32)
    conv = cb_ref[...]
    for k in range(MB_CONV):
        conv = conv + cw_ref[k:k + 1, :] * xb_ref[pl.ds(halo - MB_CONV + 1 + k, lc), :]
    xb_ref[0:halo, :] = xb_ref[lc:lc + halo, :]
    xa = _silu(conv)
    xs = xa[:, :BRANCH]
    bm = xa[:, BRANCH:BRANCH + 2 * MB_STATE]
    cm = xa[:, BRANCH + 2 * MB_STATE:]

    dt = jax.nn.softplus(dt_ref[...] + dtb_ref[...])
    a = dt * (-jnp.exp(alog_ref[...]))
    r_i = lax.broadcasted_iota(jnp.int32, (lc, lc), 0)
    c_i = lax.broadcasted_iota(jnp.int32, (lc, lc), 1)
    tril = r_i >= c_i
    cs = jnp.dot(tril.astype(F32), a, preferred_element_type=F32,
                 precision=lax.Precision.HIGHEST)
    cs_t = cs.T
    cs_last = cs[lc - 1:lc, :]

    lane = lax.broadcasted_iota(jnp.int32, (1, LANE), 1)
    left = lane < 64
    top = lax.broadcasted_iota(jnp.int32, (LANE, 1), 0) < 64

    def pair(arr, h0):
        return jnp.where(left, arr[:, h0:h0 + 1], arr[:, h0 + 1:h0 + 2])

    for g in range(2):
        bg = bm[:, g * MB_STATE:(g + 1) * MB_STATE].astype(_MXU)
        cg = cm[:, g * MB_STATE:(g + 1) * MB_STATE].astype(_MXU)
        cb = lax.dot_general(cg, bg, (((1,), (1,)), ((), ())), preferred_element_type=F32)
        for pp in range(2):
            pr = 2 * g + pp
            h0 = 2 * pr
            xp = xs[:, pr * LANE:(pr + 1) * LANE]
            xdt = xp * pair(dt, h0)
            cse = pair(cs, h0)
            xdt_m = xdt.astype(_MXU)
            ys = []
            for h in (h0, h0 + 1):
                seg = cs[:, h:h + 1] - cs_t[h:h + 1, :]
                lm = jnp.exp(jnp.where(tril, seg, -jnp.inf))
                ys.append(jnp.dot((cb * lm).astype(_MXU), xdt_m, preferred_element_type=F32))
            y = jnp.where(left, ys[0], ys[1])
            st = st_ref[pr]
            y = y + jnp.exp(cse) * lax.dot_general(cg, st.astype(_MXU), (((1,), (1,)), ((), ())),
                                                   preferred_element_type=F32)
            w_t = (xdt * jnp.exp(pair(cs_last, h0) - cse)).T.astype(_MXU)
            new = jnp.dot(w_t, bg, preferred_element_type=F32)
            dec = jnp.exp(jnp.where(top, cs_last[:, h0:h0 + 1], cs_last[:, h0 + 1:h0 + 2]))
            st_ref[pr] = st * dec + new
            y = y + d_ref[:, pr * LANE:(pr + 1) * LANE] * xp
            y = y * _silu(z_ref[:, pr * LANE:(pr + 1) * LANE].astype(F32))
            y_ref[:, pr * LANE:(pr + 1) * LANE] = y

    gw = BRANCH // 2
    for g in range(2):
        sl = slice(g * gw, (g + 1) * gw)
        o_ref[:, sl] = _rms(y_ref[:, sl], ng_ref[:, sl], gw).astype(o_ref.dtype)


def _mamba_branch(p, dt, conv_w, conv_b, dt_bias, a_log, d_exp, norm_g, bsz, s):
    t = bsz * s
    lc = min(MB_CHUNK, s)
    nc = s // lc
    cz = _COL['mb_z'] // BRANCH
    cx = _COL['mb_xbc'] // 1024
    row = lambda b, c: b * nc + c
    one = lambda b, c: (0, 0)
    return pl.pallas_call(
        functools.partial(_mamba_kernel, lc=lc),
        grid=(bsz, nc),
        in_specs=[
            pl.BlockSpec((lc, BRANCH), lambda b, c: (row(b, c), cz)),
            pl.BlockSpec((lc, 1024), lambda b, c: (row(b, c), cx)),
            pl.BlockSpec((lc, LANE), lambda b, c: (row(b, c), 0)),
            pl.BlockSpec((MB_CONV, 1024), one),
            pl.BlockSpec((1, 1024), one),
            pl.BlockSpec((1, LANE), one),
            pl.BlockSpec((1, LANE), one),
            pl.BlockSpec((1, BRANCH), one),
            pl.BlockSpec((1, BRANCH), one),
        ],
        out_specs=pl.BlockSpec((lc, BRANCH), lambda b, c: (row(b, c), 0)),
        out_shape=jax.ShapeDtypeStruct((t, BRANCH), _ACT),
        scratch_shapes=[
            pltpu.VMEM((lc + 8, 1024), F32),
            pltpu.VMEM((lc, BRANCH), F32),
            pltpu.VMEM((4, LANE, MB_STATE), F32),
        ],
        compiler_params=_params(2),
        name="mamba_ssd",
    )(p, p, dt, conv_w, conv_b, dt_bias, a_log, d_exp, norm_g)


def _s5_param_kernel(pre_ref, pim_ref, qre_ref, qim_ref, rre_ref, rim_ref, bre_ref, bim_ref,
                     cre_ref, cim_ref, kq_ref, wst_ref, wout_ref, mre_ref, mim_ref):
    l, w = S5_CHUNK, S5_GROUP
    hi = lax.Precision.HIGHEST
    nt = (((1,), (1,)), ((), ()))
    p_re, p_im = pre_ref[...], pim_ref[...]
    q_re, q_im = qre_ref[...], qim_ref[...]
    r_re, r_im = rre_ref[...], rim_ref[...]
    c_re, c_im = cre_ref[...], cim_ref[...]
    for ci in range(w):
        b_re, b_im = bre_ref[ci:ci + 1, :], bim_ref[ci:ci + 1, :]
        mre_ref[ci * l:(ci + 1) * l, :] = p_re * b_re - p_im * b_im
        mim_ref[ci * l:(ci + 1) * l, :] = p_re * b_im + p_im * b_re
        s_re = r_re * b_re - r_im * b_im
        s_im = r_re * b_im + r_im * b_re
        wst_ref[ci * l:(ci + 1) * l, :] = jnp.concatenate([s_re, s_im, s_im, s_re], axis=1).astype(_MXU)
    kq_ref[...] = (lax.dot_general(mre_ref[...], c_re, nt, precision=hi, preferred_element_type=F32)
                   - lax.dot_general(mim_ref[...], c_im, nt, precision=hi, preferred_element_type=F32))
    for co in range(w):
        g_re, g_im = c_re[co:co + 1, :], c_im[co:co + 1, :]
        n_re = q_re * g_re - q_im * g_im
        n_im = -(q_re * g_im + q_im * g_re)
        wout_ref[:, co * l:(co + 1) * l] = jnp.concatenate([n_re, n_im], axis=1).T.astype(_MXU)


def _s5_kernel(u_ref, kvec_ref, wst_ref, wout_ref, ar_ref, ai_ref, y_ref,
               toep_ref, u2_ref, sa_ref, sb_ref, hin_ref, *, nb, nc):
    l, w = S5_CHUNK, S5_GROUP
    keep = lax.broadcasted_iota(jnp.int32, (l, l), 1) >= lax.broadcasted_iota(jnp.int32, (l, l), 0)

    def build(ci, carry):
        r0 = pl.multiple_of(ci * l, l)
        for co in range(w):
            kv = jnp.broadcast_to(kvec_ref[pl.ds(ci * w + co, 1), :], (l, l))
            blk = pltpu.roll(kv, 0, 1, stride=1, stride_axis=0)
            toep_ref[pl.ds(r0, l), co * l:(co + 1) * l] = jnp.where(keep, blk, 0.0).astype(_MXU)
        return carry

    lax.fori_loop(0, w, build, 0)
    for ci in range(w):
        u2_ref[:, ci * l:(ci + 1) * l] = u_ref[ci]
    u = u2_ref[...]
    s2 = jnp.dot(u, wst_ref[...], preferred_element_type=F32)
    sa_ref[...] = s2[:, :LANE]
    sb_ref[...] = s2[:, LANE:]
    ar = ar_ref[...]
    ai = ai_ref[...]
    h = jnp.zeros((nb, LANE), F32)
    hs = h
    for c in range(nc):
        rows_c = pl.ds(c, nb, stride=nc)
        hin_ref[rows_c, :] = h
        h, hs = h * ar + hs * ai + sa_ref[rows_c, :], hs * ar - h * ai + sb_ref[rows_c, :]
    y = jnp.dot(u, toep_ref[...], preferred_element_type=F32)
    y = y + jnp.dot(hin_ref[...].astype(_MXU), wout_ref[...], preferred_element_type=F32)
    for co in range(w):
        y_ref[co] = y[:, co * l:(co + 1) * l].astype(y_ref.dtype)


def _s5_weights(lam_re, lam_im, log_step, b_re, b_im, c_re, c_im):
    g, pn, l, w = S5_GROUPS, S5_STATE, S5_CHUNK, S5_GROUP
    lr, li = lam_re.astype(F32), lam_im.astype(F32)
    step = jnp.exp(log_step.astype(F32))[:, None]
    mag = jnp.exp(lr * step)
    ab_re, ab_im = mag * jnp.cos(li * step), mag * jnp.sin(li * step)
    den = lr * lr + li * li
    f_re = ((ab_re - 1.0) * lr + ab_im * li) / den
    f_im = (ab_im * lr - (ab_re - 1.0) * li) / den
    br, bi = b_re.astype(F32), b_im.astype(F32)
    bb_re = f_re[..., None] * br - f_im[..., None] * bi
    bb_im = f_re[..., None] * bi + f_im[..., None] * br
    d = jnp.arange(l + 1, dtype=F32)[None, :, None]
    pmag = jnp.exp(d * (lr * step)[:, None, :])
    pw_re = pmag * jnp.cos(d * (li * step)[:, None, :])
    pw_im = pmag * jnp.sin(d * (li * step)[:, None, :])
    bt_re, bt_im = jnp.swapaxes(bb_re, 1, 2), jnp.swapaxes(bb_im, 1, 2)

    per = lambda *shape: pl.BlockSpec((None,) + shape, lambda i: (i, 0, 0))
    kq, wst, wout = pl.pallas_call(
        _s5_param_kernel,
        grid=(g,),
        in_specs=[per(l, pn)] * 6 + [per(w, pn)] * 4,
        out_specs=[per(w * l, w), per(w * l, 2 * LANE), per(LANE, w * l)],
        out_shape=[jax.ShapeDtypeStruct((g, w * l, w), F32),
                   jax.ShapeDtypeStruct((g, w * l, 2 * LANE), _MXU),
                   jax.ShapeDtypeStruct((g, LANE, w * l), _MXU)],
        scratch_shapes=[pltpu.VMEM((w * l, pn), F32), pltpu.VMEM((w * l, pn), F32)],
        compiler_params=_params(1),
        name="s5_params",
    )(pw_re[:, :l], pw_im[:, :l], pw_re[:, 1:], pw_im[:, 1:], pw_re[:, l - 1::-1], pw_im[:, l - 1::-1],
      bt_re, bt_im, c_re.astype(F32), c_im.astype(F32))
    kvec = kq.reshape(g, w, l, w).transpose(0, 1, 3, 2).reshape(g, w * w, l)
    ar = jnp.concatenate([pw_re[:, l], pw_re[:, l]], axis=-1)[:, None, :]
    ai = jnp.concatenate([-pw_im[:, l], pw_im[:, l]], axis=-1)[:, None, :]
    return kvec, wst, wout, ar, ai


def _s5_scan(p, weights, bsz, s):
    kvec, wst, wout, ar, ai = weights
    g, l, w = S5_GROUPS, S5_CHUNK, S5_GROUP
    t = bsz * s
    nc = s // l
    rows = bsz * nc
    u_t = lax.slice_in_dim(p, _COL['s5_u'], _COL['s5_u'] + BRANCH, axis=1).T.reshape(BRANCH, rows, l)
    per = lambda *shape: pl.BlockSpec((None,) + shape, lambda i: (i, 0, 0))
    chan = pl.BlockSpec((w, rows, l), lambda i: (i, 0, 0))
    y_t = pl.pallas_call(
        functools.partial(_s5_kernel, nb=bsz, nc=nc),
        grid=(g,),
        in_specs=[chan, per(w * w, l), per(w * l, 2 * LANE), per(LANE, w * l), per(1, LANE), per(1, LANE)],
        out_specs=chan,
        out_shape=jax.ShapeDtypeStruct((BRANCH, rows, l), _ACT),
        scratch_shapes=[
            pltpu.VMEM((w * l, w * l), _MXU),
            pltpu.VMEM((rows, w * l), _MXU),
            pltpu.VMEM((rows, LANE), F32),
            pltpu.VMEM((rows, LANE), F32),
            pltpu.VMEM((rows, LANE), F32),
        ],
        compiler_params=_params(1),
        name="s5_scan",
    )(u_t, kvec, wst, wout, ar, ai)
    return y_t.reshape(BRANCH, t).T


def _merge_kernel(ya_ref, yb_ref, ys_ref, yd_ref, u_ref, sg_ref, gates_ref, x_ref,
                  sd_ref, wglu_ref, bglu_ref, wbr_ref, wout_ref, o_ref):
    y = ys_ref[...].astype(F32) + sd_ref[...] * u_ref[...].astype(F32)
    y = 0.5 * y * (1.0 + jnp.tanh(math.sqrt(2.0 / math.pi) * (y + 0.044715 * (y * y * y))))
    y = y * jax.nn.sigmoid(jnp.dot(y.astype(_MXU), wglu_ref[...], preferred_element_type=F32) + bglu_ref[...])
    yc = y * _silu(sg_ref[...].astype(F32))
    branches = (ya_ref[...].astype(_MXU), yb_ref[...].astype(_MXU), yc.astype(_MXU), yd_ref[...].astype(_MXU))
    merged = None
    for n, br in enumerate(branches):
        gate = jax.nn.sigmoid(gates_ref[:, n * D_MODEL:(n + 1) * D_MODEL].astype(F32))
        term = gate * jnp.dot(br, wbr_ref[n], preferred_element_type=F32)
        merged = term if merged is None else merged + term
    o_ref[...] = x_ref[...] + jnp.dot(merged.astype(_MXU), wout_ref[...], preferred_element_type=F32)


def _merge(ya, yb, ys, yd, p, x2d, s5_d, w_glu, b_glu, w_br, w_out):
    t = x2d.shape[0]
    tm = min(512, t)
    cu, csg = _COL['s5_u'] // BRANCH, _COL['s5_gate'] // BRANCH
    br = pl.BlockSpec((tm, BRANCH), lambda i: (i, 0))
    return pl.pallas_call(
        _merge_kernel,
        grid=(t // tm,),
        in_specs=[
            br, br, br, br,
            pl.BlockSpec((tm, BRANCH), lambda i: (i, cu)),
            pl.BlockSpec((tm, BRANCH), lambda i: (i, csg)),
            pl.BlockSpec((tm, 4 * D_MODEL), lambda i: (i, 0)),
            pl.BlockSpec((tm, D_MODEL), lambda i: (i, 0)),
            pl.BlockSpec((1, BRANCH), lambda i: (0, 0)),
            pl.BlockSpec((BRANCH, BRANCH), lambda i: (0, 0)),
            pl.BlockSpec((1, BRANCH), lambda i: (0, 0)),
            pl.BlockSpec((4, BRANCH, D_MODEL), lambda i: (0, 0, 0)),
            pl.BlockSpec((D_MODEL, D_MODEL), lambda i: (0, 0)),
        ],
        out_specs=pl.BlockSpec((tm, D_MODEL), lambda i: (i, 0)),
        out_shape=jax.ShapeDtypeStruct((t, D_MODEL), F32),
        compiler_params=_params(1),
        name="merge_out",
    )(ya, yb, ys, yd, p, p, p, x2d, s5_d, w_glu, b_glu, w_br, w_out)


def _pack_w_in(w):
    cols = [lax.slice_in_dim(w, _SRC_OFF[n], _SRC_OFF[n] + k, axis=1) for n, k in _DST_LAYOUT]
    cols.append(jnp.zeros((D_MODEL, NP - _NP_USED), w.dtype))
    w_dt = lax.slice_in_dim(w, _SRC_OFF['mb_dt'], _SRC_OFF['mb_dt'] + MB_HEADS, axis=1)
    w_dt = jnp.pad(w_dt, ((0, 0), (0, LANE - MB_HEADS)))
    return jnp.concatenate(cols, axis=1).astype(_MXU), w_dt.astype(_MXU)


def _rope_tables(positions, rot, width, period):
    half = rot // 2
    inv_freq = 1.0 / (ROPE_THETA ** (jnp.arange(0, rot, 2, dtype=F32) / rot))
    ang = positions.astype(F32).reshape(-1, 1) * inv_freq
    cos, sin = jnp.cos(ang), jnp.sin(ang)
    t = ang.shape[0]
    pad = period - rot
    cos_p = jnp.concatenate([cos, cos, jnp.ones((t, pad), F32)], axis=1)
    sin_p = jnp.concatenate([-sin, sin, jnp.zeros((t, pad), F32)], axis=1)
    reps = width // period
    cos_p, sin_p = jnp.tile(cos_p, (1, reps)), jnp.tile(sin_p, (1, reps))
    if width < LANE:
        cos_p = jnp.concatenate([cos_p, jnp.ones((t, LANE - width), F32)], axis=1)
        sin_p = jnp.concatenate([sin_p, jnp.zeros((t, LANE - width), F32)], axis=1)
    return cos_p, sin_p


def _row(v, width=None):
    v = v.astype(F32).reshape(1, -1)
    if width is not None and v.shape[1] < width:
        v = jnp.pad(v, ((0, 0), (0, width - v.shape[1])))
    return v


def kernel(x, positions, norm_g, w_in, da_q_norm, da_k_norm, da_lambda_q1, da_lambda_k1,
           da_lambda_q2, da_lambda_k2, da_subln, mb_conv_w, mb_conv_b, mb_dt_bias, mb_a_log,
           mb_d, mb_norm, s5_lam_re, s5_lam_im, s5_log_step, s5_b_re, s5_b_im, s5_c_re,
           s5_c_im, s5_d, s5_w_glu, s5_b_glu, mla_q_a_norm, mla_w_uq, mla_kv_a_norm,
           mla_w_ukv, mla_q_norm, mla_k_norm, w_br, w_out):
    bsz, s, _ = x.shape
    depth = w_in.shape[0]
    x2d = x.reshape(bsz * s, D_MODEL)
    da_tabs = _rope_tables(positions, DA_ROT, LANE, DA_DHALF)
    mla_tabs = _rope_tables(positions, MLA_ROPE, MLA_ROPE, MLA_ROPE)
    for l in range(depth):
        lambda_init = 0.8 - 0.6 * math.exp(-0.3 * l)
        w_packed, w_dt = _pack_w_in(w_in[l])
        p, dt = _in_proj(x2d, norm_g[l], w_packed, w_dt)

        lamv = jnp.stack([da_lambda_q1[l], da_lambda_k1[l], da_lambda_q2[l], da_lambda_k2[l]]).astype(F32)
        y_a = _da_branch(p, da_tabs, _row(jnp.tile(da_q_norm[l], 2)), _row(jnp.tile(da_k_norm[l], 2)),
                         lamv, _row(da_subln[l]), lambda_init, bsz, s)

        y_b = _mamba_branch(p, dt, mb_conv_w[l].astype(F32), _row(mb_conv_b[l]),
                            _row(mb_dt_bias[l], LANE), _row(mb_a_log[l], LANE),
                            _row(jnp.repeat(mb_d[l], BRANCH // MB_HEADS)), _row(mb_norm[l]), bsz, s)

        s5w = _s5_weights(s5_lam_re[l], s5_lam_im[l], s5_log_step[l], s5_b_re[l], s5_b_im[l],
                          s5_c_re[l], s5_c_im[l])
        y_s = _s5_scan(p, s5w, bsz, s)

        wuq = mla_w_uq[l].reshape(MLA_Q_RANK, MLA_HEADS, MLA_QK)
        wuq = jnp.pad(wuq, ((0, 0), (0, 0), (0, 2 * LANE - MLA_QK))).reshape(MLA_Q_RANK, -1).astype(_MXU)
        wukv = mla_w_ukv[l].astype(_MXU)
        y_d = _mla_branch(p, mla_tabs, _row(mla_q_a_norm[l]), _row(mla_kv_a_norm[l]), wuq, wukv,
                          _row(mla_q_norm[l], 2 * LANE), _row(mla_k_norm[l], 2 * LANE), bsz, s)

        x2d = _merge(y_a, y_b, y_s, y_d, p, x2d, _row(s5_d[l]), s5_w_glu[l].astype(_MXU),
                     _row(s5_b_glu[l]), w_br[l].astype(_MXU), w_out[l].astype(_MXU))
    return x2d.reshape(bsz, s, D_MODEL)
```

```python
import functools
import math

import jax
import jax.numpy as jnp
from jax import lax
from jax.experimental import pallas as pl
from jax.experimental.pallas import tpu as pltpu

F32 = jnp.float32
_ACT = jnp.bfloat16
_MXU = jnp.bfloat16

D_MODEL = 1024
BRANCH = 512
ROPE_THETA = 500000.0
EPS = 1e-6
LOG2E = math.log2(math.e)

DA_HEADS = 4
DA_DHALF = 64
DA_ROT = 16

MB_HEADS = 8
MB_STATE = 128
MB_CONV = 4
MB_CHUNK = 128

S5_GROUP = 16
S5_GROUPS = 32
S5_STATE = 64
S5_CHUNK = 128

MLA_HEADS = 4
MLA_Q_RANK = 256
MLA_KV_RANK = 128
MLA_NOPE = 128
MLA_ROPE = 64
MLA_QK = MLA_NOPE + MLA_ROPE

LANE = 128
ATT_KEYS = 512
DA_TQ = 256
MLA_TQ = 512
VMEM_LIMIT = 48 * 1024 * 1024

_SRC_LAYOUT = (
    ('da_q', 512), ('da_k', 512), ('da_v', 512), ('da_gate', 512), ('mb_z', 512),
    ('mb_xbc', 1024), ('mb_dt', 8), ('s5_u', 512), ('s5_gate', 512), ('mla_cq', 256),
    ('mla_ckv', 128), ('mla_krope', 64), ('mla_gate', 512),
    ('gate_a', 1024), ('gate_b', 1024), ('gate_c', 1024), ('gate_d', 1024),
)
_DST_LAYOUT = (
    ('gate_a', 1024), ('gate_b', 1024), ('gate_c', 1024), ('gate_d', 1024), ('mb_xbc', 1024),
    ('da_q', 512), ('da_k', 512), ('da_v', 512), ('da_gate', 512), ('mb_z', 512),
    ('s5_u', 512), ('s5_gate', 512), ('mla_gate', 512),
    ('mla_cq', 256), ('mla_ckv', 128), ('mla_krope', 64), ('mb_dt', 8),
)


def _offsets(layout):
    out, start = {}, 0
    for name, n in layout:
        out[name] = start
        start += n
    return out, start


_SRC_OFF, _ = _offsets(_SRC_LAYOUT)
_COL, _NP_USED = _offsets(_DST_LAYOUT)
NP = 9728
MB_DT_LANE = _COL['mb_dt'] - (NP - LANE)
assert 0 < NP - _NP_USED < LANE and NP % LANE == 0 and MB_DT_LANE == MLA_ROPE


def _params(n_axes):
    return pltpu.CompilerParams(dimension_semantics=("arbitrary",) * n_axes,
                                vmem_limit_bytes=VMEM_LIMIT)


def _silu(x):
    return x * jax.nn.sigmoid(x)


def _rms(x, gain, n):
    ss = jnp.sum(x * x, axis=-1, keepdims=True)
    return x * lax.rsqrt(ss * (1.0 / n) + EPS) * gain


def _inproj_kernel(x_ref, g_ref, w_ref, p_ref, dt_ref, h_ref):
    j = pl.program_id(1)

    @pl.when(j == 0)
    def _():
        h_ref[...] = _rms(x_ref[...], g_ref[...], D_MODEL).astype(_MXU)

    acc = jnp.dot(h_ref[...], w_ref[...], preferred_element_type=F32)
    p_ref[...] = acc.astype(p_ref.dtype)

    @pl.when(j == pl.num_programs(1) - 1)
    def _():
        dt_ref[...] = acc[:, acc.shape[1] - LANE:]


def _in_proj(x2d, norm_g, w_packed):
    t = x2d.shape[0]
    tm = min(1024, t)
    tn = NP // 4
    return pl.pallas_call(
        _inproj_kernel,
        grid=(t // tm, NP // tn),
        in_specs=[
            pl.BlockSpec((tm, D_MODEL), lambda i, j: (i, 0)),
            pl.BlockSpec((1, D_MODEL), lambda i, j: (0, 0)),
            pl.BlockSpec((D_MODEL, tn), lambda i, j: (0, j)),
        ],
        out_specs=[
            pl.BlockSpec((tm, tn), lambda i, j: (i, j)),
            pl.BlockSpec((tm, LANE), lambda i, j: (i, 0)),
        ],
        out_shape=[jax.ShapeDtypeStruct((t, NP), _ACT), jax.ShapeDtypeStruct((t, LANE), F32)],
        scratch_shapes=[pltpu.VMEM((tm, D_MODEL), _MXU)],
        compiler_params=_params(2),
        name="in_proj",
    )(x2d, norm_g.reshape(1, D_MODEL), w_packed)


def _flash(qq_ref, k_ref, v_ref, m_ref, acc_ref, qi, tq, dk):
    heads, rows, _ = qq_ref.shape
    wide = ATT_KEYS
    m_ref[...] = jnp.full(m_ref.shape, -jnp.inf, F32)
    acc_ref[...] = jnp.zeros(acc_ref.shape, F32)

    def step(start, tk, masked):
        if masked:
            row = lax.broadcasted_iota(jnp.int32, (rows, tk), 0) & (tq - 1)
            col = lax.broadcasted_iota(jnp.int32, (rows, tk), 1)
            keep = col <= row
        for h in range(heads):
            k = k_ref[pl.ds(start, tk), h * dk:(h + 1) * dk]
            v = v_ref[pl.ds(start, tk), h * 2 * LANE:(h + 1) * 2 * LANE]
            s = lax.dot_general(qq_ref[h], k, (((1,), (1,)), ((), ())), preferred_element_type=F32)
            if masked:
                s = jnp.where(keep, s, -jnp.inf)
            m_prev = m_ref[h]
            m_new = jnp.maximum(m_prev, jnp.max(s, axis=-1, keepdims=True))
            alpha = jnp.exp2(m_prev - m_new)
            p = jnp.exp2(s - jnp.concatenate([m_new] * (tk // LANE), axis=1))
            acc_ref[h] = (jnp.concatenate([alpha, alpha], axis=1) * acc_ref[h]
                          + jnp.dot(p.astype(_MXU), v, preferred_element_type=F32))
            m_ref[h] = m_new

    def body(j, carry):
        step(pl.multiple_of(j * wide, wide), wide, False)
        return carry

    visible = qi * tq
    lax.fori_loop(0, visible // wide, body, 0)
    if wide > tq:
        assert wide == 2 * tq

        @pl.when(qi % 2 == 1)
        def _():
            step(pl.multiple_of(visible - tq, tq), tq, False)

    step(pl.multiple_of(visible, tq), tq, True)


def _head(h, width=LANE):
    return slice(h * width, (h + 1) * width)


def _da_kernel(q_ref, k_ref, v_ref, gate_ref, cq_ref, sq_ref, ck_ref, sk_ref, qg_ref, kg_ref,
               sub_ref, lam_ref, o_ref, kp_ref, vx_ref, qq_ref, m_ref, acc_ref, *, tq, lambda_init):
    qi = pl.program_id(1)
    lane = lax.broadcasted_iota(jnp.int32, (1, LANE), 1)
    first = lane < DA_DHALF
    rot_lo = (lane & (DA_DHALF - 1)) < DA_ROT // 2

    def prep(x, gain, cos, sin):
        x2 = x * x
        s1 = jnp.sum(jnp.where(first, x2, 0.0), axis=-1, keepdims=True)
        s2 = jnp.sum(jnp.where(first, 0.0, x2), axis=-1, keepdims=True)
        r = jnp.where(first, lax.rsqrt(s1 * (1.0 / DA_DHALF) + EPS),
                      lax.rsqrt(s2 * (1.0 / DA_DHALF) + EPS))
        y = x * r * gain
        y_sw = jnp.where(rot_lo, pltpu.roll(y, LANE - DA_ROT // 2, 1), pltpu.roll(y, DA_ROT // 2, 1))
        return y * cos + y_sw * sin

    @pl.when(qi == 0)
    def _():
        for h in range(DA_HEADS):
            kp_ref[:, _head(h)] = prep(k_ref[:, _head(h)].astype(F32), kg_ref[...], ck_ref[...],
                                       sk_ref[...]).astype(_MXU)
            vx_ref[:, 2 * h * LANE:(2 * h + 1) * LANE] = v_ref[:, _head(h)]
            vx_ref[:, (2 * h + 1) * LANE:(2 * h + 2) * LANE] = jnp.ones((vx_ref.shape[0], LANE), _MXU)

    for h in range(DA_HEADS):
        q = prep(q_ref[:, _head(h)].astype(F32), qg_ref[...], cq_ref[...], sq_ref[...]) * (DA_DHALF ** -0.5 * LOG2E)
        qq_ref[h] = jnp.concatenate([jnp.where(first, q, 0.0), jnp.where(first, 0.0, q)], axis=0).astype(_MXU)
    _flash(qq_ref, kp_ref, vx_ref, m_ref, acc_ref, qi, tq, LANE)

    lv = lam_ref[...]
    lam = (jnp.exp(jnp.sum(lv[0:1] * lv[1:2], axis=-1, keepdims=True))
           - jnp.exp(jnp.sum(lv[2:3] * lv[3:4], axis=-1, keepdims=True)) + lambda_init)
    for h in range(DA_HEADS):
        o = acc_ref[h, :, :LANE] / acc_ref[h, :, LANE:]
        d = o[:tq] - lam * o[tq:]
        y = _rms(d, sub_ref[...], LANE) * (1.0 - lambda_init)
        o_ref[:, _head(h)] = (y * _silu(gate_ref[:, _head(h)].astype(F32))).astype(o_ref.dtype)


def _da_branch(p, tabs, qg, kg, lamv, subln, lambda_init, bsz, s):
    t = bsz * s
    tq = min(DA_TQ, s)
    nq = s // tq
    cos, sin = tabs
    cq, ck, cv, cg = (_COL[n] // BRANCH for n in ('da_q', 'da_k', 'da_v', 'da_gate'))
    row = lambda b, i: b * nq + i
    one = lambda b, i: (0, 0)
    kern = functools.partial(_da_kernel, tq=tq, lambda_init=lambda_init)
    return pl.pallas_call(
        kern,
        grid=(bsz, nq),
        in_specs=[
            pl.BlockSpec((tq, BRANCH), lambda b, i: (row(b, i), cq)),
            pl.BlockSpec((s, BRANCH), lambda b, i: (b, ck)),
            pl.BlockSpec((s, BRANCH), lambda b, i: (b, cv)),
            pl.BlockSpec((tq, BRANCH), lambda b, i: (row(b, i), cg)),
            pl.BlockSpec((tq, LANE), lambda b, i: (row(b, i), 0)),
            pl.BlockSpec((tq, LANE), lambda b, i: (row(b, i), 0)),
            pl.BlockSpec((s, LANE), lambda b, i: (b, 0)),
            pl.BlockSpec((s, LANE), lambda b, i: (b, 0)),
            pl.BlockSpec((1, LANE), one),
            pl.BlockSpec((1, LANE), one),
            pl.BlockSpec((1, LANE), one),
            pl.BlockSpec((4, DA_DHALF), one),
        ],
        out_specs=pl.BlockSpec((tq, BRANCH), lambda b, i: (row(b, i), 0)),
        out_shape=jax.ShapeDtypeStruct((t, BRANCH), _ACT),
        scratch_shapes=[
            pltpu.VMEM((s, BRANCH), _MXU),
            pltpu.VMEM((s, DA_HEADS * 2 * LANE), _MXU),
            pltpu.VMEM((DA_HEADS, 2 * tq, LANE), _MXU),
            pltpu.VMEM((DA_HEADS, 2 * tq, LANE), F32),
            pltpu.VMEM((DA_HEADS, 2 * tq, 2 * LANE), F32),
        ],
        compiler_params=_params(2),
        name="diff_attention",
    )(p, p, p, p, cos, sin, cos, sin, qg, kg, subln, lamv)


def _mla_kernel(cq_ref, ckv_ref, kr_ref, gate_ref, cosq_ref, sinq_ref, cosk_ref, sink_ref,
                qan_ref, kvan_ref, wuq_ref, wukv_ref, qn_ref, kn_ref, o_ref,
                kp_ref, vp_ref, qq_ref, m_ref, acc_ref, *, tq):
    qi = pl.program_id(1)
    dk = 2 * LANE
    lane = lax.broadcasted_iota(jnp.int32, (1, LANE), 1)
    rot_lo = lane < MLA_ROPE // 2

    def rope(x, cos, sin):
        x_sw = jnp.where(rot_lo, pltpu.roll(x, LANE - MLA_ROPE // 2, 1), pltpu.roll(x, MLA_ROPE // 2, 1))
        return x * cos + x_sw * sin

    @pl.when(qi == 0)
    def _():
        c = _rms(ckv_ref[...].astype(F32), kvan_ref[...], MLA_KV_RANK).astype(_MXU)
        k_rope = jnp.where(lane < MLA_ROPE, kr_ref[...].astype(F32), 0.0)
        ss_rope = jnp.sum(k_rope * k_rope, axis=-1, keepdims=True)
        gain = kn_ref[...]
        for h in range(MLA_HEADS):
            kv = jnp.dot(c, wukv_ref[:, _head(h, dk)], preferred_element_type=F32)
            k_nope = kv[:, :MLA_NOPE]
            ss = jnp.sum(k_nope * k_nope, axis=-1, keepdims=True) + ss_rope
            r = lax.rsqrt(ss * (1.0 / MLA_QK) + EPS)
            kp_ref[:, h * dk:h * dk + MLA_NOPE] = (k_nope * r * gain[:, :MLA_NOPE]).astype(_MXU)
            kp_ref[:, h * dk + MLA_NOPE:(h + 1) * dk] = rope(
                k_rope * r * gain[:, MLA_NOPE:], cosk_ref[...], sink_ref[...]).astype(_MXU)
            vp_ref[:, 2 * h * LANE:(2 * h + 1) * LANE] = kv[:, MLA_NOPE:].astype(_MXU)
            vp_ref[:, (2 * h + 1) * LANE:(2 * h + 2) * LANE] = jnp.ones((vp_ref.shape[0], LANE), _MXU)

    c = _rms(cq_ref[...].astype(F32), qan_ref[...], MLA_Q_RANK).astype(_MXU)
    for h in range(MLA_HEADS):
        q = jnp.dot(c, wuq_ref[:, _head(h, dk)], preferred_element_type=F32)
        q = _rms(q, qn_ref[...], MLA_QK) * (MLA_QK ** -0.5 * LOG2E)
        qq_ref[h] = jnp.concatenate([q[:, :MLA_NOPE], rope(q[:, MLA_NOPE:], cosq_ref[...], sinq_ref[...])],
                                    axis=1).astype(_MXU)
    _flash(qq_ref, kp_ref, vp_ref, m_ref, acc_ref, qi, tq, dk)
    for h in range(MLA_HEADS):
        o = acc_ref[h, :, :LANE] / acc_ref[h, :, LANE:]
        o_ref[:, _head(h)] = (o * _silu(gate_ref[:, _head(h)].astype(F32))).astype(o_ref.dtype)


def _mla_branch(p, tabs, qan, kvan, wuq, wukv, qn, kn, bsz, s):
    t = bsz * s
    tq = min(MLA_TQ, s)
    nq = s // tq
    cos, sin = tabs
    ccq = _COL['mla_cq'] // MLA_Q_RANK
    cckv, ckr = (_COL[n] // LANE for n in ('mla_ckv', 'mla_krope'))
    cg = _COL['mla_gate'] // BRANCH
    row = lambda b, i: b * nq + i
    one = lambda b, i: (0, 0)
    dk = 2 * LANE
    return pl.pallas_call(
        functools.partial(_mla_kernel, tq=tq),
        grid=(bsz, nq),
        in_specs=[
            pl.BlockSpec((tq, MLA_Q_RANK), lambda b, i: (row(b, i), ccq)),
            pl.BlockSpec((s, LANE), lambda b, i: (b, cckv)),
            pl.BlockSpec((s, LANE), lambda b, i: (b, ckr)),
            pl.BlockSpec((tq, BRANCH), lambda b, i: (row(b, i), cg)),
            pl.BlockSpec((tq, LANE), lambda b, i: (row(b, i), 0)),
            pl.BlockSpec((tq, LANE), lambda b, i: (row(b, i), 0)),
            pl.BlockSpec((s, LANE), lambda b, i: (b, 0)),
            pl.BlockSpec((s, LANE), lambda b, i: (b, 0)),
            pl.BlockSpec((1, MLA_Q_RANK), one),
            pl.BlockSpec((1, MLA_KV_RANK), one),
            pl.BlockSpec((MLA_Q_RANK, MLA_HEADS * dk), one),
            pl.BlockSpec((MLA_KV_RANK, MLA_HEADS * dk), one),
            pl.BlockSpec((1, dk), one),
            pl.BlockSpec((1, dk), one),
        ],
        out_specs=pl.BlockSpec((tq, BRANCH), lambda b, i: (row(b, i), 0)),
        out_shape=jax.ShapeDtypeStruct((t, BRANCH), _ACT),
        scratch_shapes=[
            pltpu.VMEM((s, MLA_HEADS * dk), _MXU),
            pltpu.VMEM((s, MLA_HEADS * 2 * LANE), _MXU),
            pltpu.VMEM((MLA_HEADS, tq, dk), _MXU),
            pltpu.VMEM((MLA_HEADS, tq, LANE), F32),
            pltpu.VMEM((MLA_HEADS, tq, 2 * LANE), F32),
        ],
        compiler_params=_params(2),
        name="mla_attention",
    )(p, p, p, p, cos, sin, cos, sin, qan, kvan, wuq, wukv, qn, kn)


def _mamba_kernel(z_ref, xbc_ref, dt_ref, cw_ref, cb_ref, dtb_ref, alog_ref, d_ref, ng_ref,
                  o_ref, xb_ref, y_ref, st_ref, *, lc):
    halo = 8

    @pl.when(pl.program_id(1) == 0)
    def _():
        xb_ref[0:halo, :] = jnp.zeros((halo, xb_ref.shape[1]), F32)
        st_ref[...] = jnp.zeros(st_ref.shape, F32)

    xb_ref[halo:halo + lc, :] = xbc_ref[...].astype(F32)
    conv = cb_ref[...]
    for k in range(MB_CONV):
        conv = conv + cw_ref[k:k + 1, :] * xb_ref[pl.ds(halo - MB_CONV + 1 + k, lc), :]
    xb_ref[0:halo, :] = xb_ref[lc:lc + halo, :]
    xa = _silu(conv)
    xs = xa[:, :BRANCH]
    bm = xa[:, BRANCH:BRANCH + 2 * MB_STATE]
    cm = xa[:, BRANCH + 2 * MB_STATE:]

    dt = jax.nn.softplus(dt_ref[...] + dtb_ref[...])
    a = dt * (-jnp.exp(alog_ref[...]))
    r_i = lax.broadcasted_iota(jnp.int32, (lc, lc), 0)
    c_i = lax.broadcasted_iota(jnp.int32, (lc, lc), 1)
    tril = r_i >= c_i
    cs = jnp.dot(tril.astype(F32), a, preferred_element_type=F32,
                 precision=lax.Precision.HIGHEST)
    cs_t = cs.T
    cs_last = cs[lc - 1:lc, :]

    lane = lax.broadcasted_iota(jnp.int32, (1, LANE), 1)
    left = lane < 64
    top = lax.broadcasted_iota(jnp.int32, (LANE, 1), 0) < 64

    def head_col(arr, h):
        return arr[:, MB_DT_LANE + h:MB_DT_LANE + h + 1]

    def pair(arr, h0):
        return jnp.where(left, head_col(arr, h0), head_col(arr, h0 + 1))

    for g in range(2):
        bg = bm[:, g * MB_STATE:(g + 1) * MB_STATE].astype(_MXU)
        cg = cm[:, g * MB_STATE:(g + 1) * MB_STATE].astype(_MXU)
        cb = lax.dot_general(cg, bg, (((1,), (1,)), ((), ())), preferred_element_type=F32)
        for pp in range(2):
            pr = 2 * g + pp
            h0 = 2 * pr
            xp = xs[:, pr * LANE:(pr + 1) * LANE]
            xdt = xp * pair(dt, h0)
            cse = pair(cs, h0)
            xdt_m = xdt.astype(_MXU)
            ys = []
            for h in (h0, h0 + 1):
                seg = head_col(cs, h) - cs_t[MB_DT_LANE + h:MB_DT_LANE + h + 1, :]
                lm = jnp.exp(jnp.where(tril, seg, -jnp.inf))
                ys.append(jnp.dot((cb * lm).astype(_MXU), xdt_m, preferred_element_type=F32))
            y = jnp.where(left, ys[0], ys[1])
            st = st_ref[pr]
            y = y + jnp.exp(cse) * lax.dot_general(cg, st.astype(_MXU), (((1,), (1,)), ((), ())),
                                                   preferred_element_type=F32)
            w_t = (xdt * jnp.exp(pair(cs_last, h0) - cse)).T.astype(_MXU)
            new = jnp.dot(w_t, bg, preferred_element_type=F32)
            dec = jnp.exp(jnp.where(top, head_col(cs_last, h0), head_col(cs_last, h0 + 1)))
            st_ref[pr] = st * dec + new
            y = y + d_ref[:, pr * LANE:(pr + 1) * LANE] * xp
            y = y * _silu(z_ref[:, pr * LANE:(pr + 1) * LANE].astype(F32))
            y_ref[:, pr * LANE:(pr + 1) * LANE] = y

    gw = BRANCH // 2
    for g in range(2):
        sl = slice(g * gw, (g + 1) * gw)
        o_ref[:, sl] = _rms(y_ref[:, sl], ng_ref[:, sl], gw).astype(o_ref.dtype)


def _mamba_branch(p, dt, conv_w, conv_b, dt_bias, a_log, d_exp, norm_g, bsz, s):
    t = bsz * s
    lc = min(MB_CHUNK, s)
    nc = s // lc
    cz = _COL['mb_z'] // BRANCH
    cx = _COL['mb_xbc'] // 1024
    row = lambda b, c: b * nc + c
    one = lambda b, c: (0, 0)
    return pl.pallas_call(
        functools.partial(_mamba_kernel, lc=lc),
        grid=(bsz, nc),
        in_specs=[
            pl.BlockSpec((lc, BRANCH), lambda b, c: (row(b, c), cz)),
            pl.BlockSpec((lc, 1024), lambda b, c: (row(b, c), cx)),
            pl.BlockSpec((lc, LANE), lambda b, c: (row(b, c), 0)),
            pl.BlockSpec((MB_CONV, 1024), one),
            pl.BlockSpec((1, 1024), one),
            pl.BlockSpec((1, LANE), one),
            pl.BlockSpec((1, LANE), one),
            pl.BlockSpec((1, BRANCH), one),
            pl.BlockSpec((1, BRANCH), one),
        ],
        out_specs=pl.BlockSpec((lc, BRANCH), lambda b, c: (row(b, c), 0)),
        out_shape=jax.ShapeDtypeStruct((t, BRANCH), _ACT),
        scratch_shapes=[
            pltpu.VMEM((lc + 8, 1024), F32),
            pltpu.VMEM((lc, BRANCH), F32),
            pltpu.VMEM((4, LANE, MB_STATE), F32),
        ],
        compiler_params=_params(2),
        name="mamba_ssd",
    )(p, p, dt, conv_w, conv_b, dt_bias, a_log, d_exp, norm_g)


def _s5_param_kernel(pre_ref, pim_ref, qre_ref, qim_ref, rre_ref, rim_ref, bre_ref, bim_ref,
                     cre_ref, cim_ref, kq_ref, wst_ref, wout_ref, mre_ref, mim_ref):
    l, w = S5_CHUNK, S5_GROUP
    hi = lax.Precision.HIGHEST
    nt = (((1,), (1,)), ((), ()))
    p_re, p_im = pre_ref[...], pim_ref[...]
    q_re, q_im = qre_ref[...], qim_ref[...]
    r_re, r_im = rre_ref[...], rim_ref[...]
    c_re, c_im = cre_ref[...], cim_ref[...]
    for ci in range(w):
        b_re, b_im = bre_ref[ci:ci + 1, :], bim_ref[ci:ci + 1, :]
        mre_ref[ci * l:(ci + 1) * l, :] = p_re * b_re - p_im * b_im
        mim_ref[ci * l:(ci + 1) * l, :] = p_re * b_im + p_im * b_re
        s_re = r_re * b_re - r_im * b_im
        s_im = r_re * b_im + r_im * b_re
        wst_ref[ci * l:(ci + 1) * l, :] = jnp.concatenate([s_re, s_im, s_im, s_re], axis=1).astype(_MXU)
    kq_ref[...] = (lax.dot_general(mre_ref[...], c_re, nt, precision=hi, preferred_element_type=F32)
                   - lax.dot_general(mim_ref[...], c_im, nt, precision=hi, preferred_element_type=F32))
    for co in range(w):
        g_re, g_im = c_re[co:co + 1, :], c_im[co:co + 1, :]
        n_re = q_re * g_re - q_im * g_im
        n_im = -(q_re * g_im + q_im * g_re)
        wout_ref[:, co * l:(co + 1) * l] = jnp.concatenate([n_re, n_im], axis=1).T.astype(_MXU)


def _s5_kernel(u_ref, kvec_ref, wst_ref, wout_ref, ar_ref, ai_ref, y_ref,
               toep_ref, u2_ref, sa_ref, sb_ref, hin_ref, *, nb, nc):
    l, w = S5_CHUNK, S5_GROUP
    keep = lax.broadcasted_iota(jnp.int32, (l, l), 1) >= lax.broadcasted_iota(jnp.int32, (l, l), 0)

    def build(ci, carry):
        r0 = pl.multiple_of(ci * l, l)
        for co in range(w):
            kv = jnp.broadcast_to(kvec_ref[pl.ds(ci * w + co, 1), :], (l, l))
            blk = pltpu.roll(kv, 0, 1, stride=1, stride_axis=0)
            toep_ref[pl.ds(r0, l), co * l:(co + 1) * l] = jnp.where(keep, blk, 0.0).astype(_MXU)
        return carry

    lax.fori_loop(0, w, build, 0)
    for ci in range(w):
        u2_ref[:, ci * l:(ci + 1) * l] = u_ref[ci]
    u = u2_ref[...]
    s2 = jnp.dot(u, wst_ref[...], preferred_element_type=F32)
    sa_ref[...] = s2[:, :LANE]
    sb_ref[...] = s2[:, LANE:]
    ar = ar_ref[...]
    ai = ai_ref[...]
    h = jnp.zeros((nb, LANE), F32)
    hs = h
    for c in range(nc):
        rows_c = pl.ds(c, nb, stride=nc)
        hin_ref[rows_c, :] = h
        h, hs = h * ar + hs * ai + sa_ref[rows_c, :], hs * ar - h * ai + sb_ref[rows_c, :]
    y = jnp.dot(u, toep_ref[...], preferred_element_type=F32)
    y = y + jnp.dot(hin_ref[...].astype(_MXU), wout_ref[...], preferred_element_type=F32)
    for co in range(w):
        y_ref[co] = y[:, co * l:(co + 1) * l].astype(y_ref.dtype)


def _s5_weights(lam_re, lam_im, log_step, b_re, b_im, c_re, c_im):
    g, pn, l, w = S5_GROUPS, S5_STATE, S5_CHUNK, S5_GROUP
    lr, li = lam_re.astype(F32), lam_im.astype(F32)
    step = jnp.exp(log_step.astype(F32))[:, None]
    mag = jnp.exp(lr * step)
    ab_re, ab_im = mag * jnp.cos(li * step), mag * jnp.sin(li * step)
    den = lr * lr + li * li
    f_re = ((ab_re - 1.0) * lr + ab_im * li) / den
    f_im = (ab_im * lr - (ab_re - 1.0) * li) / den
    br, bi = b_re.astype(F32), b_im.astype(F32)
    bb_re = f_re[..., None] * br - f_im[..., None] * bi
    bb_im = f_re[..., None] * bi + f_im[..., None] * br
    d = jnp.arange(l + 1, dtype=F32)[None, :, None]
    pmag = jnp.exp(d * (lr * step)[:, None, :])
    pw_re = pmag * jnp.cos(d * (li * step)[:, None, :])
    pw_im = pmag * jnp.sin(d * (li * step)[:, None, :])
    bt_re, bt_im = jnp.swapaxes(bb_re, 1, 2), jnp.swapaxes(bb_im, 1, 2)

    per = lambda *shape: pl.BlockSpec((None,) + shape, lambda i: (i, 0, 0))
    kq, wst, wout = pl.pallas_call(
        _s5_param_kernel,
        grid=(g,),
        in_specs=[per(l, pn)] * 6 + [per(w, pn)] * 4,
        out_specs=[per(w * l, w), per(w * l, 2 * LANE), per(LANE, w * l)],
        out_shape=[jax.ShapeDtypeStruct((g, w * l, w), F32),
                   jax.ShapeDtypeStruct((g, w * l, 2 * LANE), _MXU),
                   jax.ShapeDtypeStruct((g, LANE, w * l), _MXU)],
        scratch_shapes=[pltpu.VMEM((w * l, pn), F32), pltpu.VMEM((w * l, pn), F32)],
        compiler_params=_params(1),
        name="s5_params",
    )(pw_re[:, :l], pw_im[:, :l], pw_re[:, 1:], pw_im[:, 1:], pw_re[:, l - 1::-1], pw_im[:, l - 1::-1],
      bt_re, bt_im, c_re.astype(F32), c_im.astype(F32))
    kvec = kq.reshape(g, w, l, w).transpose(0, 1, 3, 2).reshape(g, w * w, l)
    ar = jnp.concatenate([pw_re[:, l], pw_re[:, l]], axis=-1)[:, None, :]
    ai = jnp.concatenate([-pw_im[:, l], pw_im[:, l]], axis=-1)[:, None, :]
    return kvec, wst, wout, ar, ai


def _s5_scan(p, weights, bsz, s):
    kvec, wst, wout, ar, ai = weights
    g, l, w = S5_GROUPS, S5_CHUNK, S5_GROUP
    t = bsz * s
    nc = s // l
    rows = bsz * nc
    u_t = lax.slice_in_dim(p, _COL['s5_u'], _COL['s5_u'] + BRANCH, axis=1).T.reshape(BRANCH, rows, l)
    per = lambda *shape: pl.BlockSpec((None,) + shape, lambda i: (i, 0, 0))
    chan = pl.BlockSpec((w, rows, l), lambda i: (i, 0, 0))
    y_t = pl.pallas_call(
        functools.partial(_s5_kernel, nb=bsz, nc=nc),
        grid=(g,),
        in_specs=[chan, per(w * w, l), per(w * l, 2 * LANE), per(LANE, w * l), per(1, LANE), per(1, LANE)],
        out_specs=chan,
        out_shape=jax.ShapeDtypeStruct((BRANCH, rows, l), _ACT),
        scratch_shapes=[
            pltpu.VMEM((w * l, w * l), _MXU),
            pltpu.VMEM((rows, w * l), _MXU),
            pltpu.VMEM((rows, LANE), F32),
            pltpu.VMEM((rows, LANE), F32),
            pltpu.VMEM((rows, LANE), F32),
        ],
        compiler_params=_params(1),
        name="s5_scan",
    )(u_t, kvec, wst, wout, ar, ai)
    return y_t.reshape(BRANCH, t).T


def _merge_kernel(ya_ref, yb_ref, ys_ref, yd_ref, u_ref, sg_ref, gates_ref, x_ref,
                  sd_ref, wglu_ref, bglu_ref, wbr_ref, wout_ref, o_ref):
    y = ys_ref[...].astype(F32) + sd_ref[...] * u_ref[...].astype(F32)
    y = 0.5 * y * (1.0 + jnp.tanh(math.sqrt(2.0 / math.pi) * (y + 0.044715 * (y * y * y))))
    y = y * jax.nn.sigmoid(jnp.dot(y.astype(_MXU), wglu_ref[...], preferred_element_type=F32) + bglu_ref[...])
    yc = y * _silu(sg_ref[...].astype(F32))
    branches = (ya_ref[...].astype(_MXU), yb_ref[...].astype(_MXU), yc.astype(_MXU), yd_ref[...].astype(_MXU))
    merged = None
    for n, br in enumerate(branches):
        gate = jax.nn.sigmoid(gates_ref[:, n * D_MODEL:(n + 1) * D_MODEL].astype(F32))
        term = gate * jnp.dot(br, wbr_ref[n], preferred_element_type=F32)
        merged = term if merged is None else merged + term
    o_ref[...] = x_ref[...] + jnp.dot(merged.astype(_MXU), wout_ref[...], preferred_element_type=F32)


def _merge(ya, yb, ys, yd, p, x2d, s5_d, w_glu, b_glu, w_br, w_out):
    t = x2d.shape[0]
    tm = min(512, t)
    cu, csg = _COL['s5_u'] // BRANCH, _COL['s5_gate'] // BRANCH
    br = pl.BlockSpec((tm, BRANCH), lambda i: (i, 0))
    return pl.pallas_call(
        _merge_kernel,
        grid=(t // tm,),
        in_specs=[
            br, br, br, br,
            pl.BlockSpec((tm, BRANCH), lambda i: (i, cu)),
            pl.BlockSpec((tm, BRANCH), lambda i: (i, csg)),
            pl.BlockSpec((tm, 4 * D_MODEL), lambda i: (i, 0)),
            pl.BlockSpec((tm, D_MODEL), lambda i: (i, 0)),
            pl.BlockSpec((1, BRANCH), lambda i: (0, 0)),
            pl.BlockSpec((BRANCH, BRANCH), lambda i: (0, 0)),
            pl.BlockSpec((1, BRANCH), lambda i: (0, 0)),
            pl.BlockSpec((4, BRANCH, D_MODEL), lambda i: (0, 0, 0)),
            pl.BlockSpec((D_MODEL, D_MODEL), lambda i: (0, 0)),
        ],
        out_specs=pl.BlockSpec((tm, D_MODEL), lambda i: (i, 0)),
        out_shape=jax.ShapeDtypeStruct((t, D_MODEL), F32),
        compiler_params=_params(1),
        name="merge_out",
    )(ya, yb, ys, yd, p, p, p, x2d, s5_d, w_glu, b_glu, w_br, w_out)


def _pack_w_in(w):
    cols = [lax.slice_in_dim(w, _SRC_OFF[n], _SRC_OFF[n] + k, axis=1) for n, k in _DST_LAYOUT]
    cols.append(jnp.zeros((D_MODEL, NP - _NP_USED), w.dtype))
    return jnp.concatenate(cols, axis=1).astype(_MXU)


def _rope_tables(positions, rot, width, period):
    half = rot // 2
    inv_freq = 1.0 / (ROPE_THETA ** (jnp.arange(0, rot, 2, dtype=F32) / rot))
    ang = positions.astype(F32).reshape(-1, 1) * inv_freq
    cos, sin = jnp.cos(ang), jnp.sin(ang)
    t = ang.shape[0]
    pad = period - rot
    cos_p = jnp.concatenate([cos, cos, jnp.ones((t, pad), F32)], axis=1)
    sin_p = jnp.concatenate([-sin, sin, jnp.zeros((t, pad), F32)], axis=1)
    reps = width // period
    cos_p, sin_p = jnp.tile(cos_p, (1, reps)), jnp.tile(sin_p, (1, reps))
    if width < LANE:
        cos_p = jnp.concatenate([cos_p, jnp.ones((t, LANE - width), F32)], axis=1)
        sin_p = jnp.concatenate([sin_p, jnp.zeros((t, LANE - width), F32)], axis=1)
    return cos_p, sin_p


def _row(v, width=None, offset=0):
    v = v.astype(F32).reshape(1, -1)
    if width is not None and v.shape[1] < width:
        v = jnp.pad(v, ((0, 0), (offset, width - offset - v.shape[1])))
    return v


def kernel(x, positions, norm_g, w_in, da_q_norm, da_k_norm, da_lambda_q1, da_lambda_k1,
           da_lambda_q2, da_lambda_k2, da_subln, mb_conv_w, mb_conv_b, mb_dt_bias, mb_a_log,
           mb_d, mb_norm, s5_lam_re, s5_lam_im, s5_log_step, s5_b_re, s5_b_im, s5_c_re,
           s5_c_im, s5_d, s5_w_glu, s5_b_glu, mla_q_a_norm, mla_w_uq, mla_kv_a_norm,
           mla_w_ukv, mla_q_norm, mla_k_norm, w_br, w_out):
    bsz, s, _ = x.shape
    depth = w_in.shape[0]
    x2d = x.reshape(bsz * s, D_MODEL)
    da_tabs = _rope_tables(positions, DA_ROT, LANE, DA_DHALF)
    mla_tabs = _rope_tables(positions, MLA_ROPE, MLA_ROPE, MLA_ROPE)
    for l in range(depth):
        lambda_init = 0.8 - 0.6 * math.exp(-0.3 * l)
        p, dt = _in_proj(x2d, norm_g[l], _pack_w_in(w_in[l]))

        lamv = jnp.stack([da_lambda_q1[l], da_lambda_k1[l], da_lambda_q2[l], da_lambda_k2[l]]).astype(F32)
        y_a = _da_branch(p, da_tabs, _row(jnp.tile(da_q_norm[l], 2)), _row(jnp.tile(da_k_norm[l], 2)),
                         lamv, _row(da_subln[l]), lambda_init, bsz, s)

        y_b = _mamba_branch(p, dt, mb_conv_w[l].astype(F32), _row(mb_conv_b[l]),
                            _row(mb_dt_bias[l], LANE, MB_DT_LANE), _row(mb_a_log[l], LANE, MB_DT_LANE),
                            _row(jnp.repeat(mb_d[l], BRANCH // MB_HEADS)), _row(mb_norm[l]), bsz, s)

        s5w = _s5_weights(s5_lam_re[l], s5_lam_im[l], s5_log_step[l], s5_b_re[l], s5_b_im[l],
                          s5_c_re[l], s5_c_im[l])
        y_s = _s5_scan(p, s5w, bsz, s)

        wuq = mla_w_uq[l].reshape(MLA_Q_RANK, MLA_HEADS, MLA_QK)
        wuq = jnp.pad(wuq, ((0, 0), (0, 0), (0, 2 * LANE - MLA_QK))).reshape(MLA_Q_RANK, -1).astype(_MXU)
        wukv = mla_w_ukv[l].astype(_MXU)
        y_d = _mla_branch(p, mla_tabs, _row(mla_q_a_norm[l]), _row(mla_kv_a_norm[l]), wuq, wukv,
                          _row(mla_q_norm[l], 2 * LANE), _row(mla_k_norm[l], 2 * LANE), bsz, s)

        x2d = _merge(y_a, y_b, y_s, y_d, p, x2d, _row(s5_d[l]), s5_w_glu[l].astype(_MXU),
                     _row(s5_b_glu[l]), w_br[l].astype(_MXU), w_out[l].astype(_MXU))
    return x2d.reshape(bsz, s, D_MODEL)
```

```python
import functools
import math

import jax
import jax.numpy as jnp
from jax import lax
from jax.experimental import pallas as pl
from jax.experimental.pallas import tpu as pltpu

F32 = jnp.float32
_ACT = jnp.bfloat16
_MXU = jnp.bfloat16

D_MODEL = 1024
BRANCH = 512
ROPE_THETA = 500000.0
EPS = 1e-6
LOG2E = math.log2(math.e)

DA_HEADS = 4
DA_DHALF = 64
DA_ROT = 16

MB_HEADS = 8
MB_STATE = 128
MB_CONV = 4
MB_CHUNK = 128

S5_GROUP = 16
S5_GROUPS = 32
S5_STATE = 64
S5_CHUNK = 128

MLA_HEADS = 4
MLA_Q_RANK = 256
MLA_KV_RANK = 128
MLA_NOPE = 128
MLA_ROPE = 64
MLA_QK = MLA_NOPE + MLA_ROPE

LANE = 128
ATT_KEYS = 512
DA_TQ = 256
MLA_TQ = 512
VMEM_LIMIT = 48 * 1024 * 1024

_SRC_LAYOUT = (
    ('da_q', 512), ('da_k', 512), ('da_v', 512), ('da_gate', 512), ('mb_z', 512),
    ('mb_xbc', 1024), ('mb_dt', 8), ('s5_u', 512), ('s5_gate', 512), ('mla_cq', 256),
    ('mla_ckv', 128), ('mla_krope', 64), ('mla_gate', 512),
    ('gate_a', 1024), ('gate_b', 1024), ('gate_c', 1024), ('gate_d', 1024),
)
_DST_LAYOUT = (
    ('gate_a', 1024), ('gate_b', 1024), ('gate_c', 1024), ('gate_d', 1024), ('mb_xbc', 1024),
    ('da_q', 512), ('da_k', 512), ('da_v', 512), ('da_gate', 512), ('mb_z', 512),
    ('s5_u', 512), ('s5_gate', 512), ('mla_gate', 512),
    ('mla_cq', 256), ('mla_ckv', 128), ('mla_krope', 64), ('mb_dt', 8),
)


def _offsets(layout):
    out, start = {}, 0
    for name, n in layout:
        out[name] = start
        start += n
    return out, start


_SRC_OFF, _ = _offsets(_SRC_LAYOUT)
_COL, _NP_USED = _offsets(_DST_LAYOUT)
NP = 9728
MB_DT_LANE = _COL['mb_dt'] - (NP - LANE)
assert 0 < NP - _NP_USED < LANE and NP % LANE == 0 and MB_DT_LANE == MLA_ROPE


def _params(n_axes):
    return pltpu.CompilerParams(dimension_semantics=("arbitrary",) * n_axes,
                                vmem_limit_bytes=VMEM_LIMIT)


def _silu(x):
    return x * jax.nn.sigmoid(x)


def _rms(x, gain, n):
    ss = jnp.sum(x * x, axis=-1, keepdims=True)
    return x * lax.rsqrt(ss * (1.0 / n) + EPS) * gain


def _inproj_kernel(x_ref, g_ref, w_ref, p_ref, dt_ref, h_ref):
    j = pl.program_id(1)

    @pl.when(j == 0)
    def _():
        h_ref[...] = _rms(x_ref[...], g_ref[...], D_MODEL).astype(_MXU)

    acc = jnp.dot(h_ref[...], w_ref[...], preferred_element_type=F32)
    p_ref[...] = acc.astype(p_ref.dtype)

    @pl.when(j == pl.num_programs(1) - 1)
    def _():
        dt_ref[...] = acc[:, acc.shape[1] - LANE:]


def _in_proj(x2d, norm_g, w_packed):
    t = x2d.shape[0]
    tm = min(1024, t)
    tn = NP // 4
    return pl.pallas_call(
        _inproj_kernel,
        grid=(t // tm, NP // tn),
        in_specs=[
            pl.BlockSpec((tm, D_MODEL), lambda i, j: (i, 0)),
            pl.BlockSpec((1, D_MODEL), lambda i, j: (0, 0)),
            pl.BlockSpec((D_MODEL, tn), lambda i, j: (0, j)),
        ],
        out_specs=[
            pl.BlockSpec((tm, tn), lambda i, j: (i, j)),
            pl.BlockSpec((tm, LANE), lambda i, j: (i, 0)),
        ],
        out_shape=[jax.ShapeDtypeStruct((t, NP), _ACT), jax.ShapeDtypeStruct((t, LANE), F32)],
        scratch_shapes=[pltpu.VMEM((tm, D_MODEL), _MXU)],
        compiler_params=_params(2),
        name="in_proj",
    )(x2d, norm_g.reshape(1, D_MODEL), w_packed)


def _flash(qq_ref, k_ref, v_ref, m_ref, acc_ref, qi, tq, dk):
    heads, rows, _ = qq_ref.shape
    wide = ATT_KEYS
    m_ref[...] = jnp.full(m_ref.shape, -jnp.inf, F32)
    acc_ref[...] = jnp.zeros(acc_ref.shape, F32)

    def step(start, tk, masked):
        if masked:
            row = lax.broadcasted_iota(jnp.int32, (rows, tk), 0) & (tq - 1)
            col = lax.broadcasted_iota(jnp.int32, (rows, tk), 1)
            keep = col <= row
        for h in range(heads):
            k = k_ref[pl.ds(start, tk), h * dk:(h + 1) * dk]
            v = v_ref[pl.ds(start, tk), h * 2 * LANE:(h + 1) * 2 * LANE]
            s = lax.dot_general(qq_ref[h], k, (((1,), (1,)), ((), ())), preferred_element_type=F32)
            if masked:
                s = jnp.where(keep, s, -jnp.inf)
            m_prev = m_ref[h]
            m_new = jnp.maximum(m_prev, jnp.max(s, axis=-1, keepdims=True))
            alpha = jnp.exp2(m_prev - m_new)
            p = jnp.exp2(s - jnp.concatenate([m_new] * (tk // LANE), axis=1))
            acc_ref[h] = (jnp.concatenate([alpha, alpha], axis=1) * acc_ref[h]
                          + jnp.dot(p.astype(_MXU), v, preferred_element_type=F32))
            m_ref[h] = m_new

    def body(j, carry):
        step(pl.multiple_of(j * wide, wide), wide, False)
        return carry

    visible = qi * tq
    lax.fori_loop(0, visible // wide, body, 0)
    if wide > tq:
        assert wide == 2 * tq

        @pl.when(qi % 2 == 1)
        def _():
            step(pl.multiple_of(visible - tq, tq), tq, False)

    step(pl.multiple_of(visible, tq), tq, True)


def _head(h, width=LANE):
    return slice(h * width, (h + 1) * width)


def _da_kernel(q_ref, k_ref, v_ref, gate_ref, cq_ref, sq_ref, ck_ref, sk_ref, qg_ref, kg_ref,
               sub_ref, lam_ref, o_ref, kp_ref, vx_ref, qq_ref, m_ref, acc_ref, *, tq, lambda_init):
    qi = pl.program_id(1)
    lane = lax.broadcasted_iota(jnp.int32, (1, LANE), 1)
    first = lane < DA_DHALF
    rot_lo = (lane & (DA_DHALF - 1)) < DA_ROT // 2

    def prep(x, gain, cos, sin):
        x2 = x * x
        s1 = jnp.sum(jnp.where(first, x2, 0.0), axis=-1, keepdims=True)
        s2 = jnp.sum(jnp.where(first, 0.0, x2), axis=-1, keepdims=True)
        r = jnp.where(first, lax.rsqrt(s1 * (1.0 / DA_DHALF) + EPS),
                      lax.rsqrt(s2 * (1.0 / DA_DHALF) + EPS))
        y = x * r * gain
        y_sw = jnp.where(rot_lo, pltpu.roll(y, LANE - DA_ROT // 2, 1), pltpu.roll(y, DA_ROT // 2, 1))
        return y * cos + y_sw * sin

    @pl.when(qi == 0)
    def _():
        for h in range(DA_HEADS):
            kp_ref[:, _head(h)] = prep(k_ref[:, _head(h)].astype(F32), kg_ref[...], ck_ref[...],
                                       sk_ref[...]).astype(_MXU)
            vx_ref[:, 2 * h * LANE:(2 * h + 1) * LANE] = v_ref[:, _head(h)]
            vx_ref[:, (2 * h + 1) * LANE:(2 * h + 2) * LANE] = jnp.ones((vx_ref.shape[0], LANE), _MXU)

    for h in range(DA_HEADS):
        q = prep(q_ref[:, _head(h)].astype(F32), qg_ref[...], cq_ref[...], sq_ref[...]) * (DA_DHALF ** -0.5 * LOG2E)
        qq_ref[h] = jnp.concatenate([jnp.where(first, q, 0.0), jnp.where(first, 0.0, q)], axis=0).astype(_MXU)
    _flash(qq_ref, kp_ref, vx_ref, m_ref, acc_ref, qi, tq, LANE)

    lv = lam_ref[...]
    lam = (jnp.exp(jnp.sum(lv[0:1] * lv[1:2], axis=-1, keepdims=True))
           - jnp.exp(jnp.sum(lv[2:3] * lv[3:4], axis=-1, keepdims=True)) + lambda_init)
    for h in range(DA_HEADS):
        o = acc_ref[h, :, :LANE] / acc_ref[h, :, LANE:]
        d = o[:tq] - lam * o[tq:]
        y = _rms(d, sub_ref[...], LANE) * (1.0 - lambda_init)
        o_ref[:, _head(h)] = (y * _silu(gate_ref[:, _head(h)].astype(F32))).astype(o_ref.dtype)


def _da_branch(p, tabs, qg, kg, lamv, subln, lambda_init, bsz, s):
    t = bsz * s
    tq = min(DA_TQ, s)
    nq = s // tq
    cos, sin = tabs
    cq, ck, cv, cg = (_COL[n] // BRANCH for n in ('da_q', 'da_k', 'da_v', 'da_gate'))
    row = lambda b, i: b * nq + i
    one = lambda b, i: (0, 0)
    kern = functools.partial(_da_kernel, tq=tq, lambda_init=lambda_init)
    return pl.pallas_call(
        kern,
        grid=(bsz, nq),
        in_specs=[
            pl.BlockSpec((tq, BRANCH), lambda b, i: (row(b, i), cq)),
            pl.BlockSpec((s, BRANCH), lambda b, i: (b, ck)),
            pl.BlockSpec((s, BRANCH), lambda b, i: (b, cv)),
            pl.BlockSpec((tq, BRANCH), lambda b, i: (row(b, i), cg)),
            pl.BlockSpec((tq, LANE), lambda b, i: (row(b, i), 0)),
            pl.BlockSpec((tq, LANE), lambda b, i: (row(b, i), 0)),
            pl.BlockSpec((s, LANE), lambda b, i: (b, 0)),
            pl.BlockSpec((s, LANE), lambda b, i: (b, 0)),
            pl.BlockSpec((1, LANE), one),
            pl.BlockSpec((1, LANE), one),
            pl.BlockSpec((1, LANE), one),
            pl.BlockSpec((4, DA_DHALF), one),
        ],
        out_specs=pl.BlockSpec((tq, BRANCH), lambda b, i: (row(b, i), 0)),
        out_shape=jax.ShapeDtypeStruct((t, BRANCH), _ACT),
        scratch_shapes=[
            pltpu.VMEM((s, BRANCH), _MXU),
            pltpu.VMEM((s, DA_HEADS * 2 * LANE), _MXU),
            pltpu.VMEM((DA_HEADS, 2 * tq, LANE), _MXU),
            pltpu.VMEM((DA_HEADS, 2 * tq, LANE), F32),
            pltpu.VMEM((DA_HEADS, 2 * tq, 2 * LANE), F32),
        ],
        compiler_params=_params(2),
        name="diff_attention",
    )(p, p, p, p, cos, sin, cos, sin, qg, kg, subln, lamv)


def _mla_kernel(cq_ref, ckv_ref, kr_ref, gate_ref, cosq_ref, sinq_ref, cosk_ref, sink_ref,
                qan_ref, kvan_ref, wuq_ref, wukv_ref, qn_ref, kn_ref, o_ref,
                kp_ref, vp_ref, qq_ref, m_ref, acc_ref, *, tq):
    qi = pl.program_id(1)
    dk = 2 * LANE
    lane = lax.broadcasted_iota(jnp.int32, (1, LANE), 1)
    rot_lo = lane < MLA_ROPE // 2

    def rope(x, cos, sin):
        x_sw = jnp.where(rot_lo, pltpu.roll(x, LANE - MLA_ROPE // 2, 1), pltpu.roll(x, MLA_ROPE // 2, 1))
        return x * cos + x_sw * sin

    @pl.when(qi == 0)
    def _():
        c = _rms(ckv_ref[...].astype(F32), kvan_ref[...], MLA_KV_RANK).astype(_MXU)
        k_rope = jnp.where(lane < MLA_ROPE, kr_ref[...].astype(F32), 0.0)
        ss_rope = jnp.sum(k_rope * k_rope, axis=-1, keepdims=True)
        gain = kn_ref[...]
        for h in range(MLA_HEADS):
            kv = jnp.dot(c, wukv_ref[:, _head(h, dk)], preferred_element_type=F32)
            k_nope = kv[:, :MLA_NOPE]
            ss = jnp.sum(k_nope * k_nope, axis=-1, keepdims=True) + ss_rope
            r = lax.rsqrt(ss * (1.0 / MLA_QK) + EPS)
            kp_ref[:, h * dk:h * dk + MLA_NOPE] = (k_nope * r * gain[:, :MLA_NOPE]).astype(_MXU)
            kp_ref[:, h * dk + MLA_NOPE:(h + 1) * dk] = rope(
                k_rope * r * gain[:, MLA_NOPE:], cosk_ref[...], sink_ref[...]).astype(_MXU)
            vp_ref[:, 2 * h * LANE:(2 * h + 1) * LANE] = kv[:, MLA_NOPE:].astype(_MXU)
            vp_ref[:, (2 * h + 1) * LANE:(2 * h + 2) * LANE] = jnp.ones((vp_ref.shape[0], LANE), _MXU)

    c = _rms(cq_ref[...].astype(F32), qan_ref[...], MLA_Q_RANK).astype(_MXU)
    for h in range(MLA_HEADS):
        q = jnp.dot(c, wuq_ref[:, _head(h, dk)], preferred_element_type=F32)
        q = _rms(q, qn_ref[...], MLA_QK) * (MLA_QK ** -0.5 * LOG2E)
        qq_ref[h] = jnp.concatenate([q[:, :MLA_NOPE], rope(q[:, MLA_NOPE:], cosq_ref[...], sinq_ref[...])],
                                    axis=1).astype(_MXU)
    _flash(qq_ref, kp_ref, vp_ref, m_ref, acc_ref, qi, tq, dk)
    for h in range(MLA_HEADS):
        o = acc_ref[h, :, :LANE] / acc_ref[h, :, LANE:]
        o_ref[:, _head(h)] = (o * _silu(gate_ref[:, _head(h)].astype(F32))).astype(o_ref.dtype)


def _mla_branch(p, tabs, qan, kvan, wuq, wukv, qn, kn, bsz, s):
    t = bsz * s
    tq = min(MLA_TQ, s)
    nq = s // tq
    cos, sin = tabs
    ccq = _COL['mla_cq'] // MLA_Q_RANK
    cckv, ckr = (_COL[n] // LANE for n in ('mla_ckv', 'mla_krope'))
    cg = _COL['mla_gate'] // BRANCH
    row = lambda b, i: b * nq + i
    one = lambda b, i: (0, 0)
    dk = 2 * LANE
    return pl.pallas_call(
        functools.partial(_mla_kernel, tq=tq),
        grid=(bsz, nq),
        in_specs=[
            pl.BlockSpec((tq, MLA_Q_RANK), lambda b, i: (row(b, i), ccq)),
            pl.BlockSpec((s, LANE), lambda b, i: (b, cckv)),
            pl.BlockSpec((s, LANE), lambda b, i: (b, ckr)),
            pl.BlockSpec((tq, BRANCH), lambda b, i: (row(b, i), cg)),
            pl.BlockSpec((tq, LANE), lambda b, i: (row(b, i), 0)),
            pl.BlockSpec((tq, LANE), lambda b, i: (row(b, i), 0)),
            pl.BlockSpec((s, LANE), lambda b, i: (b, 0)),
            pl.BlockSpec((s, LANE), lambda b, i: (b, 0)),
            pl.BlockSpec((1, MLA_Q_RANK), one),
            pl.BlockSpec((1, MLA_KV_RANK), one),
            pl.BlockSpec((MLA_Q_RANK, MLA_HEADS * dk), one),
            pl.BlockSpec((MLA_KV_RANK, MLA_HEADS * dk), one),
            pl.BlockSpec((1, dk), one),
            pl.BlockSpec((1, dk), one),
        ],
        out_specs=pl.BlockSpec((tq, BRANCH), lambda b, i: (row(b, i), 0)),
        out_shape=jax.ShapeDtypeStruct((t, BRANCH), _ACT),
        scratch_shapes=[
            pltpu.VMEM((s, MLA_HEADS * dk), _MXU),
            pltpu.VMEM((s, MLA_HEADS * 2 * LANE), _MXU),
            pltpu.VMEM((MLA_HEADS, tq, dk), _MXU),
            pltpu.VMEM((MLA_HEADS, tq, LANE), F32),
            pltpu.VMEM((MLA_HEADS, tq, 2 * LANE), F32),
        ],
        compiler_params=_params(2),
        name="mla_attention",
    )(p, p, p, p, cos, sin, cos, sin, qan, kvan, wuq, wukv, qn, kn)


def _mamba_kernel(z_ref, xbc_ref, dt_ref, cw_ref, cb_ref, dtb_ref, alog_ref, d_ref, ng_ref,
                  o_ref, xb_ref, y_ref, st_ref, *, lc):
    halo = 8

    @pl.when(pl.program_id(1) == 0)
    def _():
        xb_ref[0:halo, :] = jnp.zeros((halo, xb_ref.shape[1]), F32)
        st_ref[...] = jnp.zeros(st_ref.shape, F32)

    xb_ref[halo:halo + lc, :] = xbc_ref[...].astype(F32)
    conv = cb_ref[...]
    for k in range(MB_CONV):
        conv = conv + cw_ref[k:k + 1, :] * xb_ref[pl.ds(halo - MB_CONV + 1 + k, lc), :]
    xb_ref[0:halo, :] = xb_ref[lc:lc + halo, :]
    xa = _silu(conv)
    xs = xa[:, :BRANCH]
    bm = xa[:, BRANCH:BRANCH + 2 * MB_STATE]
    cm = xa[:, BRANCH + 2 * MB_STATE:]

    dt = jax.nn.softplus(dt_ref[...] + dtb_ref[...])
    a = dt * (-jnp.exp(alog_ref[...]))
    r_i = lax.broadcasted_iota(jnp.int32, (lc, lc), 0)
    c_i = lax.broadcasted_iota(jnp.int32, (lc, lc), 1)
    tril = r_i >= c_i
    cs = jnp.dot(tril.astype(F32), a, preferred_element_type=F32,
                 precision=lax.Precision.HIGHEST)
    cs_t = cs.T
    cs_last = cs[lc - 1:lc, :]

    lane = lax.broadcasted_iota(jnp.int32, (1, LANE), 1)
    left = lane < 64
    top = lax.broadcasted_iota(jnp.int32, (LANE, 1), 0) < 64

    def head_col(arr, h):
        return arr[:, MB_DT_LANE + h:MB_DT_LANE + h + 1]

    def pair(arr, h0):
        return jnp.where(left, head_col(arr, h0), head_col(arr, h0 + 1))

    for g in range(2):
        bg = bm[:, g * MB_STATE:(g + 1) * MB_STATE].astype(_MXU)
        cg = cm[:, g * MB_STATE:(g + 1) * MB_STATE].astype(_MXU)
        cb = lax.dot_general(cg, bg, (((1,), (1,)), ((), ())), preferred_element_type=F32)
        for pp in range(2):
            pr = 2 * g + pp
            h0 = 2 * pr
            xp = xs[:, pr * LANE:(pr + 1) * LANE]
            xdt = xp * pair(dt, h0)
            cse = pair(cs, h0)
            xdt_m = xdt.astype(_MXU)
            ys = []
            for h in (h0, h0 + 1):
                seg = head_col(cs, h) - cs_t[MB_DT_LANE + h:MB_DT_LANE + h + 1, :]
                lm = jnp.exp(jnp.where(tril, seg, -jnp.inf))
                ys.append(jnp.dot((cb * lm).astype(_MXU), xdt_m, preferred_element_type=F32))
            y = jnp.where(left, ys[0], ys[1])
            st = st_ref[pr]
            y = y + jnp.exp(cse) * lax.dot_general(cg, st.astype(_MXU), (((1,), (1,)), ((), ())),
                                                   preferred_element_type=F32)
            w_t = (xdt * jnp.exp(pair(cs_last, h0) - cse)).T.astype(_MXU)
            new = jnp.dot(w_t, bg, preferred_element_type=F32)
            dec = jnp.exp(jnp.where(top, head_col(cs_last, h0), head_col(cs_last, h0 + 1)))
            st_ref[pr] = st * dec + new
            y = y + d_ref[:, pr * LANE:(pr + 1) * LANE] * xp
            y = y * _silu(z_ref[:, pr * LANE:(pr + 1) * LANE].astype(F32))
            y_ref[:, pr * LANE:(pr + 1) * LANE] = y

    gw = BRANCH // 2
    for g in range(2):
        sl = slice(g * gw, (g + 1) * gw)
        o_ref[:, sl] = _rms(y_ref[:, sl], ng_ref[:, sl], gw).astype(o_ref.dtype)


def _mamba_branch(p, dt, conv_w, conv_b, dt_bias, a_log, d_exp, norm_g, bsz, s):
    t = bsz * s
    lc = min(MB_CHUNK, s)
    nc = s // lc
    cz = _COL['mb_z'] // BRANCH
    cx = _COL['mb_xbc'] // 1024
    row = lambda b, c: b * nc + c
    one = lambda b, c: (0, 0)
    return pl.pallas_call(
        functools.partial(_mamba_kernel, lc=lc),
        grid=(bsz, nc),
        in_specs=[
            pl.BlockSpec((lc, BRANCH), lambda b, c: (row(b, c), cz)),
            pl.BlockSpec((lc, 1024), lambda b, c: (row(b, c), cx)),
            pl.BlockSpec((lc, LANE), lambda b, c: (row(b, c), 0)),
            pl.BlockSpec((MB_CONV, 1024), one),
            pl.BlockSpec((1, 1024), one),
            pl.BlockSpec((1, LANE), one),
            pl.BlockSpec((1, LANE), one),
            pl.BlockSpec((1, BRANCH), one),
            pl.BlockSpec((1, BRANCH), one),
        ],
        out_specs=pl.BlockSpec((lc, BRANCH), lambda b, c: (row(b, c), 0)),
        out_shape=jax.ShapeDtypeStruct((t, BRANCH), _ACT),
        scratch_shapes=[
            pltpu.VMEM((lc + 8, 1024), F32),
            pltpu.VMEM((lc, BRANCH), F32),
            pltpu.VMEM((4, LANE, MB_STATE), F32),
        ],
        compiler_params=_params(2),
        name="mamba_ssd",
    )(p, p, dt, conv_w, conv_b, dt_bias, a_log, d_exp, norm_g)


def _s5_param_kernel(pre_ref, pim_ref, qre_ref, qim_ref, rre_ref, rim_ref, bre_ref, bim_ref,
                     cre_ref, cim_ref, kq_ref, wst_ref, wout_ref, mre_ref, mim_ref):
    l, w = S5_CHUNK, S5_GROUP
    hi = lax.Precision.HIGHEST
    nt = (((1,), (1,)), ((), ()))
    p_re, p_im = pre_ref[...], pim_ref[...]
    q_re, q_im = qre_ref[...], qim_ref[...]
    r_re, r_im = rre_ref[...], rim_ref[...]
    c_re, c_im = cre_ref[...], cim_ref[...]
    for ci in range(w):
        b_re, b_im = bre_ref[ci:ci + 1, :], bim_ref[ci:ci + 1, :]
        mre_ref[ci * l:(ci + 1) * l, :] = p_re * b_re - p_im * b_im
        mim_ref[ci * l:(ci + 1) * l, :] = p_re * b_im + p_im * b_re
        s_re = r_re * b_re - r_im * b_im
        s_im = r_re * b_im + r_im * b_re
        wst_ref[ci * l:(ci + 1) * l, :] = jnp.concatenate([s_re, s_im, s_im, s_re], axis=1).astype(_MXU)
    kq_ref[...] = (lax.dot_general(mre_ref[...], c_re, nt, precision=hi, preferred_element_type=F32)
                   - lax.dot_general(mim_ref[...], c_im, nt, precision=hi, preferred_element_type=F32))
    for co in range(w):
        g_re, g_im = c_re[co:co + 1, :], c_im[co:co + 1, :]
        n_re = q_re * g_re - q_im * g_im
        n_im = -(q_re * g_im + q_im * g_re)
        wout_ref[:, co * l:(co + 1) * l] = jnp.concatenate([n_re, n_im], axis=1).T.astype(_MXU)


def _s5_kernel(u_ref, kvec_ref, knext_ref, wst_ref, wout_ref, ar_ref, ai_ref, y_ref,
               toep_a, toep_b, u2_ref, sa_ref, sb_ref, hin_ref, *, nb, nc):
    l, w = S5_CHUNK, S5_GROUP
    i = pl.program_id(0)
    keep = lax.broadcasted_iota(jnp.int32, (l, l), 1) >= lax.broadcasted_iota(jnp.int32, (l, l), 0)

    def build(src_ref, toep_ref):
        for ci in range(w):
            for co in range(w):
                kv = jnp.broadcast_to(src_ref[ci * w + co:ci * w + co + 1, :], (l, l))
                blk = pltpu.roll(kv, 0, 1, stride=1, stride_axis=0)
                toep_ref[ci * l:(ci + 1) * l, co * l:(co + 1) * l] = jnp.where(keep, blk, 0.0).astype(_MXU)

    def group(toep_cur, toep_next):
        for ci in range(w):
            u2_ref[:, ci * l:(ci + 1) * l] = u_ref[ci]
        u = u2_ref[...]
        s2 = jnp.dot(u, wst_ref[...], preferred_element_type=F32)
        sa_ref[...] = s2[:, :LANE]
        sb_ref[...] = s2[:, LANE:]
        ar = ar_ref[...]
        ai = ai_ref[...]
        h = jnp.zeros((nb, LANE), F32)
        hs = h
        for c in range(nc):
            rows_c = pl.ds(c, nb, stride=nc)
            hin_ref[rows_c, :] = h
            h, hs = h * ar + hs * ai + sa_ref[rows_c, :], hs * ar - h * ai + sb_ref[rows_c, :]
        build(knext_ref, toep_next)
        y = jnp.dot(u, toep_cur[...], preferred_element_type=F32)
        y = y + jnp.dot(hin_ref[...].astype(_MXU), wout_ref[...], preferred_element_type=F32)
        for co in range(w):
            y_ref[co] = y[:, co * l:(co + 1) * l].astype(y_ref.dtype)

    @pl.when(i == 0)
    def _():
        build(kvec_ref, toep_a)

    @pl.when(i % 2 == 0)
    def _():
        group(toep_a, toep_b)

    @pl.when(i % 2 == 1)
    def _():
        group(toep_b, toep_a)


def _s5_weights(lam_re, lam_im, log_step, b_re, b_im, c_re, c_im):
    g, pn, l, w = S5_GROUPS, S5_STATE, S5_CHUNK, S5_GROUP
    lr, li = lam_re.astype(F32), lam_im.astype(F32)
    step = jnp.exp(log_step.astype(F32))[:, None]
    mag = jnp.exp(lr * step)
    ab_re, ab_im = mag * jnp.cos(li * step), mag * jnp.sin(li * step)
    den = lr * lr + li * li
    f_re = ((ab_re - 1.0) * lr + ab_im * li) / den
    f_im = (ab_im * lr - (ab_re - 1.0) * li) / den
    br, bi = b_re.astype(F32), b_im.astype(F32)
    bb_re = f_re[..., None] * br - f_im[..., None] * bi
    bb_im = f_re[..., None] * bi + f_im[..., None] * br
    d = jnp.arange(l + 1, dtype=F32)[None, :, None]
    pmag = jnp.exp(d * (lr * step)[:, None, :])
    pw_re = pmag * jnp.cos(d * (li * step)[:, None, :])
    pw_im = pmag * jnp.sin(d * (li * step)[:, None, :])
    bt_re, bt_im = jnp.swapaxes(bb_re, 1, 2), jnp.swapaxes(bb_im, 1, 2)

    per = lambda *shape: pl.BlockSpec((None,) + shape, lambda i: (i, 0, 0))
    kq, wst, wout = pl.pallas_call(
        _s5_param_kernel,
        grid=(g,),
        in_specs=[per(l, pn)] * 6 + [per(w, pn)] * 4,
        out_specs=[per(w * l, w), per(w * l, 2 * LANE), per(LANE, w * l)],
        out_shape=[jax.ShapeDtypeStruct((g, w * l, w), F32),
                   jax.ShapeDtypeStruct((g, w * l, 2 * LANE), _MXU),
                   jax.ShapeDtypeStruct((g, LANE, w * l), _MXU)],
        scratch_shapes=[pltpu.VMEM((w * l, pn), F32), pltpu.VMEM((w * l, pn), F32)],
        compiler_params=_params(1),
        name="s5_params",
    )(pw_re[:, :l], pw_im[:, :l], pw_re[:, 1:], pw_im[:, 1:], pw_re[:, l - 1::-1], pw_im[:, l - 1::-1],
      bt_re, bt_im, c_re.astype(F32), c_im.astype(F32))
    kvec = kq.reshape(g, w, l, w).transpose(0, 1, 3, 2).reshape(g, w * w, l)
    ar = jnp.concatenate([pw_re[:, l], pw_re[:, l]], axis=-1)[:, None, :]
    ai = jnp.concatenate([-pw_im[:, l], pw_im[:, l]], axis=-1)[:, None, :]
    return kvec, wst, wout, ar, ai


def _s5_scan(p, weights, bsz, s):
    kvec, wst, wout, ar, ai = weights
    g, l, w = S5_GROUPS, S5_CHUNK, S5_GROUP
    t = bsz * s
    nc = s // l
    rows = bsz * nc
    u_t = lax.slice_in_dim(p, _COL['s5_u'], _COL['s5_u'] + BRANCH, axis=1).T.reshape(BRANCH, rows, l)
    per = lambda *shape: pl.BlockSpec((None,) + shape, lambda i: (i, 0, 0))
    chan = pl.BlockSpec((w, rows, l), lambda i: (i, 0, 0))
    nxt = pl.BlockSpec((None, w * w, l), lambda i: (jnp.minimum(i + 1, g - 1), 0, 0))
    y_t = pl.pallas_call(
        functools.partial(_s5_kernel, nb=bsz, nc=nc),
        grid=(g,),
        in_specs=[chan, per(w * w, l), nxt, per(w * l, 2 * LANE), per(LANE, w * l), per(1, LANE), per(1, LANE)],
        out_specs=chan,
        out_shape=jax.ShapeDtypeStruct((BRANCH, rows, l), _ACT),
        scratch_shapes=[
            pltpu.VMEM((w * l, w * l), _MXU),
            pltpu.VMEM((w * l, w * l), _MXU),
            pltpu.VMEM((rows, w * l), _MXU),
            pltpu.VMEM((rows, LANE), F32),
            pltpu.VMEM((rows, LANE), F32),
            pltpu.VMEM((rows, LANE), F32),
        ],
        compiler_params=_params(1),
        name="s5_scan",
    )(u_t, kvec, kvec, wst, wout, ar, ai)
    return y_t.reshape(BRANCH, t)


def _merge_kernel(ya_ref, yb_ref, ys_ref, yd_ref, u_ref, sg_ref, gates_ref, x_ref,
                  sd_ref, wglu_ref, bglu_ref, wbr_ref, wout_ref, o_ref):
    y = ys_ref[...].astype(F32).T + sd_ref[...] * u_ref[...].astype(F32)
    y = 0.5 * y * (1.0 + jnp.tanh(math.sqrt(2.0 / math.pi) * (y + 0.044715 * (y * y * y))))
    y = y * jax.nn.sigmoid(jnp.dot(y.astype(_MXU), wglu_ref[...], preferred_element_type=F32) + bglu_ref[...])
    yc = y * _silu(sg_ref[...].astype(F32))
    branches = (ya_ref[...].astype(_MXU), yb_ref[...].astype(_MXU), yc.astype(_MXU), yd_ref[...].astype(_MXU))
    merged = None
    for n, br in enumerate(branches):
        gate = jax.nn.sigmoid(gates_ref[:, n * D_MODEL:(n + 1) * D_MODEL].astype(F32))
        term = gate * jnp.dot(br, wbr_ref[n], preferred_element_type=F32)
        merged = term if merged is None else merged + term
    o_ref[...] = x_ref[...] + jnp.dot(merged.astype(_MXU), wout_ref[...], preferred_element_type=F32)


def _merge(ya, yb, ys_t, yd, p, x2d, s5_d, w_glu, b_glu, w_br, w_out):
    t = x2d.shape[0]
    tm = min(512, t)
    cu, csg = _COL['s5_u'] // BRANCH, _COL['s5_gate'] // BRANCH
    br = pl.BlockSpec((tm, BRANCH), lambda i: (i, 0))
    return pl.pallas_call(
        _merge_kernel,
        grid=(t // tm,),
        in_specs=[
            br, br, pl.BlockSpec((BRANCH, tm), lambda i: (0, i)), br,
            pl.BlockSpec((tm, BRANCH), lambda i: (i, cu)),
            pl.BlockSpec((tm, BRANCH), lambda i: (i, csg)),
            pl.BlockSpec((tm, 4 * D_MODEL), lambda i: (i, 0)),
            pl.BlockSpec((tm, D_MODEL), lambda i: (i, 0)),
            pl.BlockSpec((1, BRANCH), lambda i: (0, 0)),
            pl.BlockSpec((BRANCH, BRANCH), lambda i: (0, 0)),
            pl.BlockSpec((1, BRANCH), lambda i: (0, 0)),
            pl.BlockSpec((4, BRANCH, D_MODEL), lambda i: (0, 0, 0)),
            pl.BlockSpec((D_MODEL, D_MODEL), lambda i: (0, 0)),
        ],
        out_specs=pl.BlockSpec((tm, D_MODEL), lambda i: (i, 0)),
        out_shape=jax.ShapeDtypeStruct((t, D_MODEL), F32),
        compiler_params=_params(1),
        name="merge_out",
    )(ya, yb, ys_t, yd, p, p, p, x2d, s5_d, w_glu, b_glu, w_br, w_out)


def _pack_w_in(w):
    cols = [lax.slice_in_dim(w, _SRC_OFF[n], _SRC_OFF[n] + k, axis=1) for n, k in _DST_LAYOUT]
    cols.append(jnp.zeros((D_MODEL, NP - _NP_USED), w.dtype))
    return jnp.concatenate(cols, axis=1).astype(_MXU)


def _rope_tables(positions, rot, width, period):
    half = rot // 2
    inv_freq = 1.0 / (ROPE_THETA ** (jnp.arange(0, rot, 2, dtype=F32) / rot))
    ang = positions.astype(F32).reshape(-1, 1) * inv_freq
    cos, sin = jnp.cos(ang), jnp.sin(ang)
    t = ang.shape[0]
    pad = period - rot
    cos_p = jnp.concatenate([cos, cos, jnp.ones((t, pad), F32)], axis=1)
    sin_p = jnp.concatenate([-sin, sin, jnp.zeros((t, pad), F32)], axis=1)
    reps = width // period
    cos_p, sin_p = jnp.tile(cos_p, (1, reps)), jnp.tile(sin_p, (1, reps))
    if width < LANE:
        cos_p = jnp.concatenate([cos_p, jnp.ones((t, LANE - width), F32)], axis=1)
        sin_p = jnp.concatenate([sin_p, jnp.zeros((t, LANE - width), F32)], axis=1)
    return cos_p, sin_p


def _row(v, width=None, offset=0):
    v = v.astype(F32).reshape(1, -1)
    if width is not None and v.shape[1] < width:
        v = jnp.pad(v, ((0, 0), (offset, width - offset - v.shape[1])))
    return v


def kernel(x, positions, norm_g, w_in, da_q_norm, da_k_norm, da_lambda_q1, da_lambda_k1,
           da_lambda_q2, da_lambda_k2, da_subln, mb_conv_w, mb_conv_b, mb_dt_bias, mb_a_log,
           mb_d, mb_norm, s5_lam_re, s5_lam_im, s5_log_step, s5_b_re, s5_b_im, s5_c_re,
           s5_c_im, s5_d, s5_w_glu, s5_b_glu, mla_q_a_norm, mla_w_uq, mla_kv_a_norm,
           mla_w_ukv, mla_q_norm, mla_k_norm, w_br, w_out):
    bsz, s, _ = x.shape
    depth = w_in.shape[0]
    x2d = x.reshape(bsz * s, D_MODEL)
    da_tabs = _rope_tables(positions, DA_ROT, LANE, DA_DHALF)
    mla_tabs = _rope_tables(positions, MLA_ROPE, MLA_ROPE, MLA_ROPE)
    for l in range(depth):
        lambda_init = 0.8 - 0.6 * math.exp(-0.3 * l)
        p, dt = _in_proj(x2d, norm_g[l], _pack_w_in(w_in[l]))

        lamv = jnp.stack([da_lambda_q1[l], da_lambda_k1[l], da_lambda_q2[l], da_lambda_k2[l]]).astype(F32)
        y_a = _da_branch(p, da_tabs, _row(jnp.tile(da_q_norm[l], 2)), _row(jnp.tile(da_k_norm[l], 2)),
                         lamv, _row(da_subln[l]), lambda_init, bsz, s)

        y_b = _mamba_branch(p, dt, mb_conv_w[l].astype(F32), _row(mb_conv_b[l]),
                            _row(mb_dt_bias[l], LANE, MB_DT_LANE), _row(mb_a_log[l], LANE, MB_DT_LANE),
                            _row(jnp.repeat(mb_d[l], BRANCH // MB_HEADS)), _row(mb_norm[l]), bsz, s)

        s5w = _s5_weights(s5_lam_re[l], s5_lam_im[l], s5_log_step[l], s5_b_re[l], s5_b_im[l],
                          s5_c_re[l], s5_c_im[l])
        y_s = _s5_scan(p, s5w, bsz, s)

        wuq = mla_w_uq[l].reshape(MLA_Q_RANK, MLA_HEADS, MLA_QK)
        wuq = jnp.pad(wuq, ((0, 0), (0, 0), (0, 2 * LANE - MLA_QK))).reshape(MLA_Q_RANK, -1).astype(_MXU)
        wukv = mla_w_ukv[l].astype(_MXU)
        y_d = _mla_branch(p, mla_tabs, _row(mla_q_a_norm[l]), _row(mla_kv_a_norm[l]), wuq, wukv,
                          _row(mla_q_norm[l], 2 * LANE), _row(mla_k_norm[l], 2 * LANE), bsz, s)

        x2d = _merge(y_a, y_b, y_s, y_d, p, x2d, _row(s5_d[l]), s5_w_glu[l].astype(_MXU),
                     _row(s5_b_glu[l]), w_br[l].astype(_MXU), w_out[l].astype(_MXU))
    return x2d.reshape(bsz, s, D_MODEL)
```

```python
import functools
import math

import jax
import jax.numpy as jnp
from jax import lax
from jax.experimental import pallas as pl
from jax.experimental.pallas import tpu as pltpu

F32 = jnp.float32
_ACT = jnp.bfloat16
_MXU = jnp.bfloat16

D_MODEL = 1024
BRANCH = 512
ROPE_THETA = 500000.0
EPS = 1e-6
LOG2E = math.log2(math.e)

DA_HEADS = 4
DA_DHALF = 64
DA_ROT = 16

MB_HEADS = 8
MB_STATE = 128
MB_CONV = 4
MB_CHUNK = 128

S5_GROUP = 16
S5_GROUPS = 32
S5_STATE = 64
S5_CHUNK = 128

MLA_HEADS = 4
MLA_Q_RANK = 256
MLA_KV_RANK = 128
MLA_NOPE = 128
MLA_ROPE = 64
MLA_QK = MLA_NOPE + MLA_ROPE

LANE = 128
ATT_KEYS = 512
DA_TQ = 256
MLA_TQ = 512
VMEM_LIMIT = 48 * 1024 * 1024

_SRC_LAYOUT = (
    ('da_q', 512), ('da_k', 512), ('da_v', 512), ('da_gate', 512), ('mb_z', 512),
    ('mb_xbc', 1024), ('mb_dt', 8), ('s5_u', 512), ('s5_gate', 512), ('mla_cq', 256),
    ('mla_ckv', 128), ('mla_krope', 64), ('mla_gate', 512),
    ('gate_a', 1024), ('gate_b', 1024), ('gate_c', 1024), ('gate_d', 1024),
)
_DST_LAYOUT = (
    ('gate_a', 1024), ('gate_b', 1024), ('gate_c', 1024), ('gate_d', 1024), ('mb_xbc', 1024),
    ('da_q', 512), ('da_k', 512), ('da_v', 512), ('da_gate', 512), ('mb_z', 512),
    ('s5_u', 512), ('s5_gate', 512), ('mla_gate', 512),
    ('mla_cq', 256), ('mla_ckv', 128), ('mla_krope', 64), ('mb_dt', 8),
)


def _offsets(layout):
    out, start = {}, 0
    for name, n in layout:
        out[name] = start
        start += n
    return out, start


_SRC_OFF, _ = _offsets(_SRC_LAYOUT)
_COL, _NP_USED = _offsets(_DST_LAYOUT)
NP = 9728
MB_DT_LANE = _COL['mb_dt'] - (NP - LANE)
assert 0 < NP - _NP_USED < LANE and NP % LANE == 0 and MB_DT_LANE == MLA_ROPE


def _params(n_axes):
    return pltpu.CompilerParams(dimension_semantics=("arbitrary",) * n_axes,
                                vmem_limit_bytes=VMEM_LIMIT)


def _silu(x):
    return x * jax.nn.sigmoid(x)


def _rms(x, gain, n):
    ss = jnp.sum(x * x, axis=-1, keepdims=True)
    return x * lax.rsqrt(ss * (1.0 / n) + EPS) * gain


def _inproj_kernel(x_ref, g_ref, w_ref, p_ref, dt_ref, h_ref):
    j = pl.program_id(1)

    @pl.when(j == 0)
    def _():
        h_ref[...] = _rms(x_ref[...], g_ref[...], D_MODEL).astype(_MXU)

    acc = jnp.dot(h_ref[...], w_ref[...], preferred_element_type=F32)
    p_ref[...] = acc.astype(p_ref.dtype)

    @pl.when(j == pl.num_programs(1) - 1)
    def _():
        dt_ref[...] = acc[:, acc.shape[1] - LANE:]


def _in_proj(x2d, norm_g, w_packed):
    t = x2d.shape[0]
    tm = min(1024, t)
    tn = NP // 4
    return pl.pallas_call(
        _inproj_kernel,
        grid=(t // tm, NP // tn),
        in_specs=[
            pl.BlockSpec((tm, D_MODEL), lambda i, j: (i, 0)),
            pl.BlockSpec((1, D_MODEL), lambda i, j: (0, 0)),
            pl.BlockSpec((D_MODEL, tn), lambda i, j: (0, j)),
        ],
        out_specs=[
            pl.BlockSpec((tm, tn), lambda i, j: (i, j)),
            pl.BlockSpec((tm, LANE), lambda i, j: (i, 0)),
        ],
        out_shape=[jax.ShapeDtypeStruct((t, NP), _ACT), jax.ShapeDtypeStruct((t, LANE), F32)],
        scratch_shapes=[pltpu.VMEM((tm, D_MODEL), _MXU)],
        compiler_params=_params(2),
        name="in_proj",
    )(x2d, norm_g.reshape(1, D_MODEL), w_packed)


def _flash(qq_ref, k_ref, v_ref, m_ref, acc_ref, qi, tq, dk):
    heads, rows, _ = qq_ref.shape
    wide = ATT_KEYS
    m_ref[...] = jnp.full(m_ref.shape, -jnp.inf, F32)
    acc_ref[...] = jnp.zeros(acc_ref.shape, F32)

    def step(start, tk, masked):
        if masked:
            row = lax.broadcasted_iota(jnp.int32, (rows, tk), 0) & (tq - 1)
            col = lax.broadcasted_iota(jnp.int32, (rows, tk), 1)
            keep = col <= row
        for h in range(heads):
            k = k_ref[pl.ds(start, tk), h * dk:(h + 1) * dk]
            v = v_ref[pl.ds(start, tk), h * 2 * LANE:(h + 1) * 2 * LANE]
            s = lax.dot_general(qq_ref[h], k, (((1,), (1,)), ((), ())), preferred_element_type=F32)
            if masked:
                s = jnp.where(keep, s, -jnp.inf)
            m_prev = m_ref[h]
            m_new = jnp.maximum(m_prev, jnp.max(s, axis=-1, keepdims=True))
            alpha = jnp.exp2(m_prev - m_new)
            p = jnp.exp2(s - jnp.concatenate([m_new] * (tk // LANE), axis=1))
            acc_ref[h] = (jnp.concatenate([alpha, alpha], axis=1) * acc_ref[h]
                          + jnp.dot(p.astype(_MXU), v, preferred_element_type=F32))
            m_ref[h] = m_new

    def body(j, carry):
        step(pl.multiple_of(j * wide, wide), wide, False)
        return carry

    visible = qi * tq
    lax.fori_loop(0, visible // wide, body, 0)
    if wide > tq:
        assert wide == 2 * tq

        @pl.when(qi % 2 == 1)
        def _():
            step(pl.multiple_of(visible - tq, tq), tq, False)

    step(pl.multiple_of(visible, tq), tq, True)


def _head(h, width=LANE):
    return slice(h * width, (h + 1) * width)


def _da_kernel(q_ref, k_ref, v_ref, gate_ref, cq_ref, sq_ref, ck_ref, sk_ref, qg_ref, kg_ref,
               sub_ref, lam_ref, gsum_ref, o_ref, kp_ref, vx_ref, qq_ref, m_ref, acc_ref, *, tq, lambda_init):
    qi = pl.program_id(1)
    lane = lax.broadcasted_iota(jnp.int32, (1, LANE), 1)
    first = lane < DA_DHALF
    rot_lo = (lane & (DA_DHALF - 1)) < DA_ROT // 2

    def prep(x, gain, cos, sin):
        ss = jnp.dot((x * x).astype(_MXU), gsum_ref[...], preferred_element_type=F32)
        y = x * lax.rsqrt(ss * (1.0 / DA_DHALF) + EPS)
        heads = []
        for half in range(2):
            yh = y[:, _head(half)] * gain
            yh_sw = jnp.where(rot_lo, pltpu.roll(yh, LANE - DA_ROT // 2, 1), pltpu.roll(yh, DA_ROT // 2, 1))
            heads.append(yh * cos + yh_sw * sin)
        return heads

    @pl.when(qi == 0)
    def _():
        for pair in range(DA_HEADS // 2):
            ks = prep(k_ref[:, _head(pair, 2 * LANE)].astype(F32), kg_ref[...], ck_ref[...], sk_ref[...])
            for half, k in enumerate(ks):
                kp_ref[:, _head(2 * pair + half)] = k.astype(_MXU)
        for h in range(DA_HEADS):
            vx_ref[:, 2 * h * LANE:(2 * h + 1) * LANE] = v_ref[:, _head(h)]
            vx_ref[:, (2 * h + 1) * LANE:(2 * h + 2) * LANE] = jnp.ones((vx_ref.shape[0], LANE), _MXU)

    for pair in range(DA_HEADS // 2):
        qs = prep(q_ref[:, _head(pair, 2 * LANE)].astype(F32), qg_ref[...], cq_ref[...], sq_ref[...])
        for half, q in enumerate(qs):
            q = q * (DA_DHALF ** -0.5 * LOG2E)
            qq_ref[2 * pair + half] = jnp.concatenate(
                [jnp.where(first, q, 0.0), jnp.where(first, 0.0, q)], axis=0).astype(_MXU)
    _flash(qq_ref, kp_ref, vx_ref, m_ref, acc_ref, qi, tq, LANE)

    lv = lam_ref[...]
    lam = (jnp.exp(jnp.sum(lv[0:1] * lv[1:2], axis=-1, keepdims=True))
           - jnp.exp(jnp.sum(lv[2:3] * lv[3:4], axis=-1, keepdims=True)) + lambda_init)
    for h in range(DA_HEADS):
        o = acc_ref[h, :, :LANE] / acc_ref[h, :, LANE:]
        d = o[:tq] - lam * o[tq:]
        y = _rms(d, sub_ref[...], LANE) * (1.0 - lambda_init)
        o_ref[:, _head(h)] = (y * _silu(gate_ref[:, _head(h)].astype(F32))).astype(o_ref.dtype)


def _da_branch(p, tabs, qg, kg, lamv, subln, lambda_init, bsz, s):
    t = bsz * s
    tq = min(DA_TQ, s)
    nq = s // tq
    cos, sin = tabs
    cq, ck, cv, cg = (_COL[n] // BRANCH for n in ('da_q', 'da_k', 'da_v', 'da_gate'))
    row = lambda b, i: b * nq + i
    one = lambda b, i: (0, 0)
    kern = functools.partial(_da_kernel, tq=tq, lambda_init=lambda_init)
    map_id = jnp.arange(2 * LANE) // DA_DHALF
    map_sum = (map_id[:, None] == map_id[None, :]).astype(_MXU)
    return pl.pallas_call(
        kern,
        grid=(bsz, nq),
        in_specs=[
            pl.BlockSpec((tq, BRANCH), lambda b, i: (row(b, i), cq)),
            pl.BlockSpec((s, BRANCH), lambda b, i: (b, ck)),
            pl.BlockSpec((s, BRANCH), lambda b, i: (b, cv)),
            pl.BlockSpec((tq, BRANCH), lambda b, i: (row(b, i), cg)),
            pl.BlockSpec((tq, LANE), lambda b, i: (row(b, i), 0)),
            pl.BlockSpec((tq, LANE), lambda b, i: (row(b, i), 0)),
            pl.BlockSpec((s, LANE), lambda b, i: (b, 0)),
            pl.BlockSpec((s, LANE), lambda b, i: (b, 0)),
            pl.BlockSpec((1, LANE), one),
            pl.BlockSpec((1, LANE), one),
            pl.BlockSpec((1, LANE), one),
            pl.BlockSpec((4, DA_DHALF), one),
            pl.BlockSpec((2 * LANE, 2 * LANE), one),
        ],
        out_specs=pl.BlockSpec((tq, BRANCH), lambda b, i: (row(b, i), 0)),
        out_shape=jax.ShapeDtypeStruct((t, BRANCH), _ACT),
        scratch_shapes=[
            pltpu.VMEM((s, BRANCH), _MXU),
            pltpu.VMEM((s, DA_HEADS * 2 * LANE), _MXU),
            pltpu.VMEM((DA_HEADS, 2 * tq, LANE), _MXU),
            pltpu.VMEM((DA_HEADS, 2 * tq, LANE), F32),
            pltpu.VMEM((DA_HEADS, 2 * tq, 2 * LANE), F32),
        ],
        compiler_params=_params(2),
        name="diff_attention",
    )(p, p, p, p, cos, sin, cos, sin, qg, kg, subln, lamv, map_sum)


def _mla_kernel(cq_ref, ckv_ref, kr_ref, gate_ref, cosq_ref, sinq_ref, cosk_ref, sink_ref,
                qan_ref, kvan_ref, wuq_ref, wukv_ref, qn_ref, kn_ref, o_ref,
                kp_ref, vp_ref, qq_ref, m_ref, acc_ref, *, tq):
    qi = pl.program_id(1)
    dk = 2 * LANE
    lane = lax.broadcasted_iota(jnp.int32, (1, LANE), 1)
    rot_lo = lane < MLA_ROPE // 2

    def rope(x, cos, sin):
        x_sw = jnp.where(rot_lo, pltpu.roll(x, LANE - MLA_ROPE // 2, 1), pltpu.roll(x, MLA_ROPE // 2, 1))
        return x * cos + x_sw * sin

    @pl.when(qi == 0)
    def _():
        c = _rms(ckv_ref[...].astype(F32), kvan_ref[...], MLA_KV_RANK).astype(_MXU)
        k_rope = jnp.where(lane < MLA_ROPE, kr_ref[...].astype(F32), 0.0)
        ss_rope = jnp.sum(k_rope * k_rope, axis=-1, keepdims=True)
        gain = kn_ref[...]
        for h in range(MLA_HEADS):
            kv = jnp.dot(c, wukv_ref[:, _head(h, dk)], preferred_element_type=F32)
            k_nope = kv[:, :MLA_NOPE]
            ss = jnp.sum(k_nope * k_nope, axis=-1, keepdims=True) + ss_rope
            r = lax.rsqrt(ss * (1.0 / MLA_QK) + EPS)
            kp_ref[:, h * dk:h * dk + MLA_NOPE] = (k_nope * r * gain[:, :MLA_NOPE]).astype(_MXU)
            kp_ref[:, h * dk + MLA_NOPE:(h + 1) * dk] = rope(
                k_rope * r * gain[:, MLA_NOPE:], cosk_ref[...], sink_ref[...]).astype(_MXU)
            vp_ref[:, 2 * h * LANE:(2 * h + 1) * LANE] = kv[:, MLA_NOPE:].astype(_MXU)
            vp_ref[:, (2 * h + 1) * LANE:(2 * h + 2) * LANE] = jnp.ones((vp_ref.shape[0], LANE), _MXU)

    c = _rms(cq_ref[...].astype(F32), qan_ref[...], MLA_Q_RANK).astype(_MXU)
    for h in range(MLA_HEADS):
        q = jnp.dot(c, wuq_ref[:, _head(h, dk)], preferred_element_type=F32)
        q = _rms(q, qn_ref[...], MLA_QK) * (MLA_QK ** -0.5 * LOG2E)
        qq_ref[h] = jnp.concatenate([q[:, :MLA_NOPE], rope(q[:, MLA_NOPE:], cosq_ref[...], sinq_ref[...])],
                                    axis=1).astype(_MXU)
    _flash(qq_ref, kp_ref, vp_ref, m_ref, acc_ref, qi, tq, dk)
    for h in range(MLA_HEADS):
        o = acc_ref[h, :, :LANE] / acc_ref[h, :, LANE:]
        o_ref[:, _head(h)] = (o * _silu(gate_ref[:, _head(h)].astype(F32))).astype(o_ref.dtype)


def _mla_branch(p, tabs, qan, kvan, wuq, wukv, qn, kn, bsz, s):
    t = bsz * s
    tq = min(MLA_TQ, s)
    nq = s // tq
    cos, sin = tabs
    ccq = _COL['mla_cq'] // MLA_Q_RANK
    cckv, ckr = (_COL[n] // LANE for n in ('mla_ckv', 'mla_krope'))
    cg = _COL['mla_gate'] // BRANCH
    row = lambda b, i: b * nq + i
    one = lambda b, i: (0, 0)
    dk = 2 * LANE
    return pl.pallas_call(
        functools.partial(_mla_kernel, tq=tq),
        grid=(bsz, nq),
        in_specs=[
            pl.BlockSpec((tq, MLA_Q_RANK), lambda b, i: (row(b, i), ccq)),
            pl.BlockSpec((s, LANE), lambda b, i: (b, cckv)),
            pl.BlockSpec((s, LANE), lambda b, i: (b, ckr)),
            pl.BlockSpec((tq, BRANCH), lambda b, i: (row(b, i), cg)),
            pl.BlockSpec((tq, LANE), lambda b, i: (row(b, i), 0)),
            pl.BlockSpec((tq, LANE), lambda b, i: (row(b, i), 0)),
            pl.BlockSpec((s, LANE), lambda b, i: (b, 0)),
            pl.BlockSpec((s, LANE), lambda b, i: (b, 0)),
            pl.BlockSpec((1, MLA_Q_RANK), one),
            pl.BlockSpec((1, MLA_KV_RANK), one),
            pl.BlockSpec((MLA_Q_RANK, MLA_HEADS * dk), one),
            pl.BlockSpec((MLA_KV_RANK, MLA_HEADS * dk), one),
            pl.BlockSpec((1, dk), one),
            pl.BlockSpec((1, dk), one),
        ],
        out_specs=pl.BlockSpec((tq, BRANCH), lambda b, i: (row(b, i), 0)),
        out_shape=jax.ShapeDtypeStruct((t, BRANCH), _ACT),
        scratch_shapes=[
            pltpu.VMEM((s, MLA_HEADS * dk), _MXU),
            pltpu.VMEM((s, MLA_HEADS * 2 * LANE), _MXU),
            pltpu.VMEM((MLA_HEADS, tq, dk), _MXU),
            pltpu.VMEM((MLA_HEADS, tq, LANE), F32),
            pltpu.VMEM((MLA_HEADS, tq, 2 * LANE), F32),
        ],
        compiler_params=_params(2),
        name="mla_attention",
    )(p, p, p, p, cos, sin, cos, sin, qan, kvan, wuq, wukv, qn, kn)


def _mamba_kernel(z_ref, xbc_ref, dt_ref, cw_ref, cb_ref, dtb_ref, alog_ref, d_ref, ng_ref,
                  o_ref, xb_ref, y_ref, st_ref, *, lc):
    halo = 8

    @pl.when(pl.program_id(1) == 0)
    def _():
        xb_ref[0:halo, :] = jnp.zeros((halo, xb_ref.shape[1]), F32)
        st_ref[...] = jnp.zeros(st_ref.shape, F32)

    xb_ref[halo:halo + lc, :] = xbc_ref[...].astype(F32)
    conv = cb_ref[...]
    for k in range(MB_CONV):
        conv = conv + cw_ref[k:k + 1, :] * xb_ref[pl.ds(halo - MB_CONV + 1 + k, lc), :]
    xb_ref[0:halo, :] = xb_ref[lc:lc + halo, :]
    xa = _silu(conv)
    xs = xa[:, :BRANCH]
    bm = xa[:, BRANCH:BRANCH + 2 * MB_STATE]
    cm = xa[:, BRANCH + 2 * MB_STATE:]

    dt = jax.nn.softplus(dt_ref[...] + dtb_ref[...])
    a = dt * (-jnp.exp(alog_ref[...]))
    r_i = lax.broadcasted_iota(jnp.int32, (lc, lc), 0)
    c_i = lax.broadcasted_iota(jnp.int32, (lc, lc), 1)
    tril = r_i >= c_i
    cs = jnp.dot(tril.astype(F32), a, preferred_element_type=F32,
                 precision=lax.Precision.HIGHEST)
    cs_t = cs.T
    cs_last = cs[lc - 1:lc, :]

    lane = lax.broadcasted_iota(jnp.int32, (1, LANE), 1)
    left = lane < 64
    top = lax.broadcasted_iota(jnp.int32, (LANE, 1), 0) < 64

    def head_col(arr, h):
        return arr[:, MB_DT_LANE + h:MB_DT_LANE + h + 1]

    def pair(arr, h0):
        return jnp.where(left, head_col(arr, h0), head_col(arr, h0 + 1))

    for g in range(2):
        bg = bm[:, g * MB_STATE:(g + 1) * MB_STATE].astype(_MXU)
        cg = cm[:, g * MB_STATE:(g + 1) * MB_STATE].astype(_MXU)
        cb = lax.dot_general(cg, bg, (((1,), (1,)), ((), ())), preferred_element_type=F32)
        for pp in range(2):
            pr = 2 * g + pp
            h0 = 2 * pr
            xp = xs[:, pr * LANE:(pr + 1) * LANE]
            xdt = xp * pair(dt, h0)
            cse = pair(cs, h0)
            xdt_m = xdt.astype(_MXU)
            ys = []
            for h in (h0, h0 + 1):
                seg = head_col(cs, h) - cs_t[MB_DT_LANE + h:MB_DT_LANE + h + 1, :]
                lm = jnp.exp(jnp.where(tril, seg, -jnp.inf))
                ys.append(jnp.dot((cb * lm).astype(_MXU), xdt_m, preferred_element_type=F32))
            y = jnp.where(left, ys[0], ys[1])
            st = st_ref[pr]
            y = y + jnp.exp(cse) * lax.dot_general(cg, st.astype(_MXU), (((1,), (1,)), ((), ())),
                                                   preferred_element_type=F32)
            w_t = (xdt * jnp.exp(pair(cs_last, h0) - cse)).T.astype(_MXU)
            new = jnp.dot(w_t, bg, preferred_element_type=F32)
            dec = jnp.exp(jnp.where(top, head_col(cs_last, h0), head_col(cs_last, h0 + 1)))
            st_ref[pr] = st * dec + new
            y = y + d_ref[:, pr * LANE:(pr + 1) * LANE] * xp
            y = y * _silu(z_ref[:, pr * LANE:(pr + 1) * LANE].astype(F32))
            y_ref[:, pr * LANE:(pr + 1) * LANE] = y

    gw = BRANCH // 2
    for g in range(2):
        sl = slice(g * gw, (g + 1) * gw)
        o_ref[:, sl] = _rms(y_ref[:, sl], ng_ref[:, sl], gw).astype(o_ref.dtype)


def _mamba_branch(p, dt, conv_w, conv_b, dt_bias, a_log, d_exp, norm_g, bsz, s):
    t = bsz * s
    lc = min(MB_CHUNK, s)
    nc = s // lc
    cz = _COL['mb_z'] // BRANCH
    cx = _COL['mb_xbc'] // 1024
    row = lambda b, c: b * nc + c
    one = lambda b, c: (0, 0)
    return pl.pallas_call(
        functools.partial(_mamba_kernel, lc=lc),
        grid=(bsz, nc),
        in_specs=[
            pl.BlockSpec((lc, BRANCH), lambda b, c: (row(b, c), cz)),
            pl.BlockSpec((lc, 1024), lambda b, c: (row(b, c), cx)),
            pl.BlockSpec((lc, LANE), lambda b, c: (row(b, c), 0)),
            pl.BlockSpec((MB_CONV, 1024), one),
            pl.BlockSpec((1, 1024), one),
            pl.BlockSpec((1, LANE), one),
            pl.BlockSpec((1, LANE), one),
            pl.BlockSpec((1, BRANCH), one),
            pl.BlockSpec((1, BRANCH), one),
        ],
        out_specs=pl.BlockSpec((lc, BRANCH), lambda b, c: (row(b, c), 0)),
        out_shape=jax.ShapeDtypeStruct((t, BRANCH), _ACT),
        scratch_shapes=[
            pltpu.VMEM((lc + 8, 1024), F32),
            pltpu.VMEM((lc, BRANCH), F32),
            pltpu.VMEM((4, LANE, MB_STATE), F32),
        ],
        compiler_params=_params(2),
        name="mamba_ssd",
    )(p, p, dt, conv_w, conv_b, dt_bias, a_log, d_exp, norm_g)


def _s5_param_kernel(pre_ref, pim_ref, qre_ref, qim_ref, rre_ref, rim_ref, bre_ref, bim_ref,
                     cre_ref, cim_ref, kq_ref, wst_ref, wout_ref, mre_ref, mim_ref):
    l, w = S5_CHUNK, S5_GROUP
    hi = lax.Precision.HIGHEST
    nt = (((1,), (1,)), ((), ()))
    p_re, p_im = pre_ref[...], pim_ref[...]
    q_re, q_im = qre_ref[...], qim_ref[...]
    r_re, r_im = rre_ref[...], rim_ref[...]
    c_re, c_im = cre_ref[...], cim_ref[...]
    for ci in range(w):
        b_re, b_im = bre_ref[ci:ci + 1, :], bim_ref[ci:ci + 1, :]
        mre_ref[ci * l:(ci + 1) * l, :] = p_re * b_re - p_im * b_im
        mim_ref[ci * l:(ci + 1) * l, :] = p_re * b_im + p_im * b_re
        s_re = r_re * b_re - r_im * b_im
        s_im = r_re * b_im + r_im * b_re
        wst_ref[ci * l:(ci + 1) * l, :] = jnp.concatenate([s_re, s_im, s_im, s_re], axis=1).astype(_MXU)
    kq_ref[...] = (lax.dot_general(mre_ref[...], c_re, nt, precision=hi, preferred_element_type=F32)
                   - lax.dot_general(mim_ref[...], c_im, nt, precision=hi, preferred_element_type=F32))
    for co in range(w):
        g_re, g_im = c_re[co:co + 1, :], c_im[co:co + 1, :]
        n_re = q_re * g_re - q_im * g_im
        n_im = -(q_re * g_im + q_im * g_re)
        wout_ref[:, co * l:(co + 1) * l] = jnp.concatenate([n_re, n_im], axis=1).T.astype(_MXU)


def _s5_kernel(u_ref, kvec_ref, knext_ref, wst_ref, wout_ref, ar_ref, ai_ref, y_ref,
               toep_a, toep_b, u2_ref, sa_ref, sb_ref, hin_ref, *, nb, nc):
    l, w = S5_CHUNK, S5_GROUP
    i = pl.program_id(0)
    keep = lax.broadcasted_iota(jnp.int32, (l, l), 1) >= lax.broadcasted_iota(jnp.int32, (l, l), 0)

    def build(src_ref, toep_ref):
        for ci in range(w):
            for co in range(w):
                kv = jnp.broadcast_to(src_ref[ci * w + co:ci * w + co + 1, :], (l, l))
                blk = pltpu.roll(kv, 0, 1, stride=1, stride_axis=0)
                toep_ref[ci * l:(ci + 1) * l, co * l:(co + 1) * l] = jnp.where(keep, blk, 0.0).astype(_MXU)

    def group(toep_cur, toep_next):
        for ci in range(w):
            u2_ref[:, ci * l:(ci + 1) * l] = u_ref[ci]
        u = u2_ref[...]
        s2 = jnp.dot(u, wst_ref[...], preferred_element_type=F32)
        sa_ref[...] = s2[:, :LANE]
        sb_ref[...] = s2[:, LANE:]
        ar = ar_ref[...]
        ai = ai_ref[...]
        h = jnp.zeros((nb, LANE), F32)
        hs = h
        for c in range(nc):
            rows_c = pl.ds(c, nb, stride=nc)
            hin_ref[rows_c, :] = h
            h, hs = h * ar + hs * ai + sa_ref[rows_c, :], hs * ar - h * ai + sb_ref[rows_c, :]
        build(knext_ref, toep_next)
        y = jnp.dot(u, toep_cur[...], preferred_element_type=F32)
        y = y + jnp.dot(hin_ref[...].astype(_MXU), wout_ref[...], preferred_element_type=F32)
        for co in range(w):
            y_ref[co] = y[:, co * l:(co + 1) * l].astype(y_ref.dtype)

    @pl.when(i == 0)
    def _():
        build(kvec_ref, toep_a)

    @pl.when(i % 2 == 0)
    def _():
        group(toep_a, toep_b)

    @pl.when(i % 2 == 1)
    def _():
        group(toep_b, toep_a)


def _s5_weights(lam_re, lam_im, log_step, b_re, b_im, c_re, c_im):
    g, pn, l, w = S5_GROUPS, S5_STATE, S5_CHUNK, S5_GROUP
    lr, li = lam_re.astype(F32), lam_im.astype(F32)
    step = jnp.exp(log_step.astype(F32))[:, None]
    mag = jnp.exp(lr * step)
    ab_re, ab_im = mag * jnp.cos(li * step), mag * jnp.sin(li * step)
    den = lr * lr + li * li
    f_re = ((ab_re - 1.0) * lr + ab_im * li) / den
    f_im = (ab_im * lr - (ab_re - 1.0) * li) / den
    br, bi = b_re.astype(F32), b_im.astype(F32)
    bb_re = f_re[..., None] * br - f_im[..., None] * bi
    bb_im = f_re[..., None] * bi + f_im[..., None] * br
    d = jnp.arange(l + 1, dtype=F32)[None, :, None]
    pmag = jnp.exp(d * (lr * step)[:, None, :])
    pw_re = pmag * jnp.cos(d * (li * step)[:, None, :])
    pw_im = pmag * jnp.sin(d * (li * step)[:, None, :])
    bt_re, bt_im = jnp.swapaxes(bb_re, 1, 2), jnp.swapaxes(bb_im, 1, 2)

    per = lambda *shape: pl.BlockSpec((None,) + shape, lambda i: (i, 0, 0))
    kq, wst, wout = pl.pallas_call(
        _s5_param_kernel,
        grid=(g,),
        in_specs=[per(l, pn)] * 6 + [per(w, pn)] * 4,
        out_specs=[per(w * l, w), per(w * l, 2 * LANE), per(LANE, w * l)],
        out_shape=[jax.ShapeDtypeStruct((g, w * l, w), F32),
                   jax.ShapeDtypeStruct((g, w * l, 2 * LANE), _MXU),
                   jax.ShapeDtypeStruct((g, LANE, w * l), _MXU)],
        scratch_shapes=[pltpu.VMEM((w * l, pn), F32), pltpu.VMEM((w * l, pn), F32)],
        compiler_params=_params(1),
        name="s5_params",
    )(pw_re[:, :l], pw_im[:, :l], pw_re[:, 1:], pw_im[:, 1:], pw_re[:, l - 1::-1], pw_im[:, l - 1::-1],
      bt_re, bt_im, c_re.astype(F32), c_im.astype(F32))
    kvec = kq.reshape(g, w, l, w).transpose(0, 1, 3, 2).reshape(g, w * w, l)
    ar = jnp.concatenate([pw_re[:, l], pw_re[:, l]], axis=-1)[:, None, :]
    ai = jnp.concatenate([-pw_im[:, l], pw_im[:, l]], axis=-1)[:, None, :]
    return kvec, wst, wout, ar, ai


def _s5_scan(p, weights, bsz, s):
    kvec, wst, wout, ar, ai = weights
    g, l, w = S5_GROUPS, S5_CHUNK, S5_GROUP
    t = bsz * s
    nc = s // l
    rows = bsz * nc
    u_t = lax.slice_in_dim(p, _COL['s5_u'], _COL['s5_u'] + BRANCH, axis=1).T.reshape(BRANCH, rows, l)
    per = lambda *shape: pl.BlockSpec((None,) + shape, lambda i: (i, 0, 0))
    chan = pl.BlockSpec((w, rows, l), lambda i: (i, 0, 0))
    nxt = pl.BlockSpec((None, w * w, l), lambda i: (jnp.minimum(i + 1, g - 1), 0, 0))
    y_t = pl.pallas_call(
        functools.partial(_s5_kernel, nb=bsz, nc=nc),
        grid=(g,),
        in_specs=[chan, per(w * w, l), nxt, per(w * l, 2 * LANE), per(LANE, w * l), per(1, LANE), per(1, LANE)],
        out_specs=chan,
        out_shape=jax.ShapeDtypeStruct((BRANCH, rows, l), _ACT),
        scratch_shapes=[
            pltpu.VMEM((w * l, w * l), _MXU),
            pltpu.VMEM((w * l, w * l), _MXU),
            pltpu.VMEM((rows, w * l), _MXU),
            pltpu.VMEM((rows, LANE), F32),
            pltpu.VMEM((rows, LANE), F32),
            pltpu.VMEM((rows, LANE), F32),
        ],
        compiler_params=_params(1),
        name="s5_scan",
    )(u_t, kvec, kvec, wst, wout, ar, ai)
    return y_t.reshape(BRANCH, t).T


def _merge_kernel(ya_ref, yb_ref, ys_ref, yd_ref, u_ref, sg_ref, gates_ref, x_ref,
                  sd_ref, wglu_ref, bglu_ref, wbr_ref, wout_ref, o_ref):
    y = ys_ref[...].astype(F32) + sd_ref[...] * u_ref[...].astype(F32)
    y = 0.5 * y * (1.0 + jnp.tanh(math.sqrt(2.0 / math.pi) * (y + 0.044715 * (y * y * y))))
    y = y * jax.nn.sigmoid(jnp.dot(y.astype(_MXU), wglu_ref[...], preferred_element_type=F32) + bglu_ref[...])
    yc = y * _silu(sg_ref[...].astype(F32))
    branches = (ya_ref[...].astype(_MXU), yb_ref[...].astype(_MXU), yc.astype(_MXU), yd_ref[...].astype(_MXU))
    merged = None
    for n, br in enumerate(branches):
        gate = jax.nn.sigmoid(gates_ref[:, n * D_MODEL:(n + 1) * D_MODEL].astype(F32))
        term = gate * jnp.dot(br, wbr_ref[n], preferred_element_type=F32)
        merged = term if merged is None else merged + term
    o_ref[...] = x_ref[...] + jnp.dot(merged.astype(_MXU), wout_ref[...], preferred_element_type=F32)


def _merge(ya, yb, ys, yd, p, x2d, s5_d, w_glu, b_glu, w_br, w_out):
    t = x2d.shape[0]
    tm = min(512, t)
    cu, csg = _COL['s5_u'] // BRANCH, _COL['s5_gate'] // BRANCH
    br = pl.BlockSpec((tm, BRANCH), lambda i: (i, 0))
    return pl.pallas_call(
        _merge_kernel,
        grid=(t // tm,),
        in_specs=[
            br, br, br, br,
            pl.BlockSpec((tm, BRANCH), lambda i: (i, cu)),
            pl.BlockSpec((tm, BRANCH), lambda i: (i, csg)),
            pl.BlockSpec((tm, 4 * D_MODEL), lambda i: (i, 0)),
            pl.BlockSpec((tm, D_MODEL), lambda i: (i, 0)),
            pl.BlockSpec((1, BRANCH), lambda i: (0, 0)),
            pl.BlockSpec((BRANCH, BRANCH), lambda i: (0, 0)),
            pl.BlockSpec((1, BRANCH), lambda i: (0, 0)),
            pl.BlockSpec((4, BRANCH, D_MODEL), lambda i: (0, 0, 0)),
            pl.BlockSpec((D_MODEL, D_MODEL), lambda i: (0, 0)),
        ],
        out_specs=pl.BlockSpec((tm, D_MODEL), lambda i: (i, 0)),
        out_shape=jax.ShapeDtypeStruct((t, D_MODEL), F32),
        compiler_params=_params(1),
        name="merge_out",
    )(ya, yb, ys, yd, p, p, p, x2d, s5_d, w_glu, b_glu, w_br, w_out)


def _pack_w_in(w):
    cols = [lax.slice_in_dim(w, _SRC_OFF[n], _SRC_OFF[n] + k, axis=1) for n, k in _DST_LAYOUT]
    cols.append(jnp.zeros((D_MODEL, NP - _NP_USED), w.dtype))
    return jnp.concatenate(cols, axis=1).astype(_MXU)


def _rope_tables(positions, rot, width, period):
    half = rot // 2
    inv_freq = 1.0 / (ROPE_THETA ** (jnp.arange(0, rot, 2, dtype=F32) / rot))
    ang = positions.astype(F32).reshape(-1, 1) * inv_freq
    cos, sin = jnp.cos(ang), jnp.sin(ang)
    t = ang.shape[0]
    pad = period - rot
    cos_p = jnp.concatenate([cos, cos, jnp.ones((t, pad), F32)], axis=1)
    sin_p = jnp.concatenate([-sin, sin, jnp.zeros((t, pad), F32)], axis=1)
    reps = width // period
    cos_p, sin_p = jnp.tile(cos_p, (1, reps)), jnp.tile(sin_p, (1, reps))
    if width < LANE:
        cos_p = jnp.concatenate([cos_p, jnp.ones((t, LANE - width), F32)], axis=1)
        sin_p = jnp.concatenate([sin_p, jnp.zeros((t, LANE - width), F32)], axis=1)
    return cos_p, sin_p


def _row(v, width=None, offset=0):
    v = v.astype(F32).reshape(1, -1)
    if width is not None and v.shape[1] < width:
        v = jnp.pad(v, ((0, 0), (offset, width - offset - v.shape[1])))
    return v


def kernel(x, positions, norm_g, w_in, da_q_norm, da_k_norm, da_lambda_q1, da_lambda_k1,
           da_lambda_q2, da_lambda_k2, da_subln, mb_conv_w, mb_conv_b, mb_dt_bias, mb_a_log,
           mb_d, mb_norm, s5_lam_re, s5_lam_im, s5_log_step, s5_b_re, s5_b_im, s5_c_re,
           s5_c_im, s5_d, s5_w_glu, s5_b_glu, mla_q_a_norm, mla_w_uq, mla_kv_a_norm,
           mla_w_ukv, mla_q_norm, mla_k_norm, w_br, w_out):
    bsz, s, _ = x.shape
    depth = w_in.shape[0]
    x2d = x.reshape(bsz * s, D_MODEL)
    da_tabs = _rope_tables(positions, DA_ROT, LANE, DA_DHALF)
    mla_tabs = _rope_tables(positions, MLA_ROPE, MLA_ROPE, MLA_ROPE)
    for l in range(depth):
        lambda_init = 0.8 - 0.6 * math.exp(-0.3 * l)
        p, dt = _in_proj(x2d, norm_g[l], _pack_w_in(w_in[l]))

        lamv = jnp.stack([da_lambda_q1[l], da_lambda_k1[l], da_lambda_q2[l], da_lambda_k2[l]]).astype(F32)
        y_a = _da_branch(p, da_tabs, _row(jnp.tile(da_q_norm[l], 2)), _row(jnp.tile(da_k_norm[l], 2)),
                         lamv, _row(da_subln[l]), lambda_init, bsz, s)

        y_b = _mamba_branch(p, dt, mb_conv_w[l].astype(F32), _row(mb_conv_b[l]),
                            _row(mb_dt_bias[l], LANE, MB_DT_LANE), _row(mb_a_log[l], LANE, MB_DT_LANE),
                            _row(jnp.repeat(mb_d[l], BRANCH // MB_HEADS)), _row(mb_norm[l]), bsz, s)

        s5w = _s5_weights(s5_lam_re[l], s5_lam_im[l], s5_log_step[l], s5_b_re[l], s5_b_im[l],
                          s5_c_re[l], s5_c_im[l])
        y_s = _s5_scan(p, s5w, bsz, s)

        wuq = mla_w_uq[l].reshape(MLA_Q_RANK, MLA_HEADS, MLA_QK)
        wuq = jnp.pad(wuq, ((0, 0), (0, 0), (0, 2 * LANE - MLA_QK))).reshape(MLA_Q_RANK, -1).astype(_MXU)
        wukv = mla_w_ukv[l].astype(_MXU)
        y_d = _mla_branch(p, mla_tabs, _row(mla_q_a_norm[l]), _row(mla_kv_a_norm[l]), wuq, wukv,
                          _row(mla_q_norm[l], 2 * LANE), _row(mla_k_norm[l], 2 * LANE), bsz, s)

        x2d = _merge(y_a, y_b, y_s, y_d, p, x2d, _row(s5_d[l]), s5_w_glu[l].astype(_MXU),
                     _row(s5_b_glu[l]), w_br[l].astype(_MXU), w_out[l].astype(_MXU))
    return x2d.reshape(bsz, s, D_MODEL)
```

```python
import functools
import math

import jax
import jax.numpy as jnp
from jax import lax
from jax.experimental import pallas as pl
from jax.experimental.pallas import tpu as pltpu

F32 = jnp.float32
_ACT = jnp.bfloat16
_MXU = jnp.bfloat16

D_MODEL = 1024
BRANCH = 512
ROPE_THETA = 500000.0
EPS = 1e-6
LOG2E = math.log2(math.e)

DA_HEADS = 4
DA_DHALF = 64
DA_ROT = 16

MB_HEADS = 8
MB_STATE = 128
MB_CONV = 4
MB_CHUNK = 128

S5_GROUP = 16
S5_GROUPS = 32
S5_STATE = 64
S5_CHUNK = 128

MLA_HEADS = 4
MLA_Q_RANK = 256
MLA_KV_RANK = 128
MLA_NOPE = 128
MLA_ROPE = 64
MLA_QK = MLA_NOPE + MLA_ROPE

LANE = 128
ATT_KEYS = 512
DA_TQ = 256
MLA_TQ = 512
VMEM_LIMIT = 48 * 1024 * 1024

_SRC_LAYOUT = (
    ('da_q', 512), ('da_k', 512), ('da_v', 512), ('da_gate', 512), ('mb_z', 512),
    ('mb_xbc', 1024), ('mb_dt', 8), ('s5_u', 512), ('s5_gate', 512), ('mla_cq', 256),
    ('mla_ckv', 128), ('mla_krope', 64), ('mla_gate', 512),
    ('gate_a', 1024), ('gate_b', 1024), ('gate_c', 1024), ('gate_d', 1024),
)
_DST_LAYOUT = (
    ('gate_a', 1024), ('gate_b', 1024), ('gate_c', 1024), ('gate_d', 1024), ('mb_xbc', 1024),
    ('da_q', 512), ('da_k', 512), ('da_v', 512), ('da_gate', 512), ('mb_z', 512),
    ('s5_u', 512), ('s5_gate', 512), ('mla_gate', 512),
    ('mla_cq', 256), ('mla_ckv', 128), ('mla_krope', 64), ('mb_dt', 8),
)


def _offsets(layout):
    out, start = {}, 0
    for name, n in layout:
        out[name] = start
        start += n
    return out, start


_SRC_OFF, _ = _offsets(_SRC_LAYOUT)
_COL, _NP_USED = _offsets(_DST_LAYOUT)
NP = 9728
MB_DT_LANE = _COL['mb_dt'] - (NP - LANE)
assert 0 < NP - _NP_USED < LANE and NP % LANE == 0 and MB_DT_LANE == MLA_ROPE


def _params(n_axes):
    return pltpu.CompilerParams(dimension_semantics=("arbitrary",) * n_axes,
                                vmem_limit_bytes=VMEM_LIMIT)


def _silu(x):
    return x * jax.nn.sigmoid(x)


def _rms(x, gain, n):
    ss = jnp.sum(x * x, axis=-1, keepdims=True)
    return x * lax.rsqrt(ss * (1.0 / n) + EPS) * gain


def _inproj_kernel(x_ref, g_ref, w_ref, p_ref, dt_ref, h_ref):
    j = pl.program_id(1)

    @pl.when(j == 0)
    def _():
        h_ref[...] = _rms(x_ref[...], g_ref[...], D_MODEL).astype(_MXU)

    acc = jnp.dot(h_ref[...], w_ref[...], preferred_element_type=F32)
    p_ref[...] = acc.astype(p_ref.dtype)

    @pl.when(j == pl.num_programs(1) - 1)
    def _():
        dt_ref[...] = acc[:, acc.shape[1] - LANE:]


def _in_proj(x2d, norm_g, w_packed):
    t = x2d.shape[0]
    tm = min(1024, t)
    tn = NP // 4
    return pl.pallas_call(
        _inproj_kernel,
        grid=(t // tm, NP // tn),
        in_specs=[
            pl.BlockSpec((tm, D_MODEL), lambda i, j: (i, 0)),
            pl.BlockSpec((1, D_MODEL), lambda i, j: (0, 0)),
            pl.BlockSpec((D_MODEL, tn), lambda i, j: (0, j)),
        ],
        out_specs=[
            pl.BlockSpec((tm, tn), lambda i, j: (i, j)),
            pl.BlockSpec((tm, LANE), lambda i, j: (i, 0)),
        ],
        out_shape=[jax.ShapeDtypeStruct((t, NP), _ACT), jax.ShapeDtypeStruct((t, LANE), F32)],
        scratch_shapes=[pltpu.VMEM((tm, D_MODEL), _MXU)],
        compiler_params=_params(2),
        name="in_proj",
    )(x2d, norm_g.reshape(1, D_MODEL), w_packed)


def _flash(qq_ref, k_ref, v_ref, m_ref, acc_ref, qi, tq, dk):
    heads, rows, _ = qq_ref.shape
    wide = ATT_KEYS
    m_ref[...] = jnp.full(m_ref.shape, -jnp.inf, F32)
    acc_ref[...] = jnp.zeros(acc_ref.shape, F32)

    def step(start, tk, masked):
        if masked:
            row = lax.broadcasted_iota(jnp.int32, (rows, tk), 0) & (tq - 1)
            col = lax.broadcasted_iota(jnp.int32, (rows, tk), 1)
            keep = col <= row
        for h in range(heads):
            k = k_ref[pl.ds(start, tk), h * dk:(h + 1) * dk]
            v = v_ref[pl.ds(start, tk), h * 2 * LANE:(h + 1) * 2 * LANE]
            s = lax.dot_general(qq_ref[h], k, (((1,), (1,)), ((), ())), preferred_element_type=F32)
            if masked:
                s = jnp.where(keep, s, -jnp.inf)
            m_prev = m_ref[h]
            m_new = jnp.maximum(m_prev, jnp.max(s, axis=-1, keepdims=True))
            alpha = jnp.exp2(m_prev - m_new)
            p = jnp.exp2(s - jnp.concatenate([m_new] * (tk // LANE), axis=1))
            acc_ref[h] = (jnp.concatenate([alpha, alpha], axis=1) * acc_ref[h]
                          + jnp.dot(p.astype(_MXU), v, preferred_element_type=F32))
            m_ref[h] = m_new

    def body(j, carry):
        step(pl.multiple_of(j * wide, wide), wide, False)
        return carry

    visible = qi * tq
    lax.fori_loop(0, visible // wide, body, 0)
    if wide > tq:
        assert wide == 2 * tq

        @pl.when(qi % 2 == 1)
        def _():
            step(pl.multiple_of(visible - tq, tq), tq, False)

    step(pl.multiple_of(visible, tq), tq, True)


def _head(h, width=LANE):
    return slice(h * width, (h + 1) * width)


def _da_kernel(q_ref, k_ref, v_ref, gate_ref, cq_ref, sq_ref, ck_ref, sk_ref, qg_ref, kg_ref,
               sub_ref, lam_ref, gsum_ref, o_ref, kp_ref, vx_ref, qq_ref, m_ref, acc_ref, *, tq, lambda_init):
    qi = pl.program_id(1)
    lane = lax.broadcasted_iota(jnp.int32, (1, LANE), 1)
    first = lane < DA_DHALF
    rot_lo = (lane & (DA_DHALF - 1)) < DA_ROT // 2

    def prep(x, gain, cos, sin):
        ss = jnp.dot((x * x).astype(_MXU), gsum_ref[...], preferred_element_type=F32)
        y = x * lax.rsqrt(ss * (1.0 / DA_DHALF) + EPS)
        heads = []
        for half in range(2):
            yh = y[:, _head(half)] * gain
            yh_sw = jnp.where(rot_lo, pltpu.roll(yh, LANE - DA_ROT // 2, 1), pltpu.roll(yh, DA_ROT // 2, 1))
            heads.append(yh * cos + yh_sw * sin)
        return heads

    @pl.when(qi == 0)
    def _():
        for pair in range(DA_HEADS // 2):
            ks = prep(k_ref[:, _head(pair, 2 * LANE)].astype(F32), kg_ref[...], ck_ref[...], sk_ref[...])
            for half, k in enumerate(ks):
                kp_ref[:, _head(2 * pair + half)] = k.astype(_MXU)
        for h in range(DA_HEADS):
            vx_ref[:, 2 * h * LANE:(2 * h + 1) * LANE] = v_ref[:, _head(h)]
            vx_ref[:, (2 * h + 1) * LANE:(2 * h + 2) * LANE] = jnp.ones((vx_ref.shape[0], LANE), _MXU)

    for pair in range(DA_HEADS // 2):
        qs = prep(q_ref[:, _head(pair, 2 * LANE)].astype(F32), qg_ref[...], cq_ref[...], sq_ref[...])
        for half, q in enumerate(qs):
            q = q * (DA_DHALF ** -0.5 * LOG2E)
            qq_ref[2 * pair + half] = jnp.concatenate(
                [jnp.where(first, q, 0.0), jnp.where(first, 0.0, q)], axis=0).astype(_MXU)
    _flash(qq_ref, kp_ref, vx_ref, m_ref, acc_ref, qi, tq, LANE)

    lv = lam_ref[...]
    lam = (jnp.exp(jnp.sum(lv[0:1] * lv[1:2], axis=-1, keepdims=True))
           - jnp.exp(jnp.sum(lv[2:3] * lv[3:4], axis=-1, keepdims=True)) + lambda_init)
    for h in range(DA_HEADS):
        o = acc_ref[h, :, :LANE] / acc_ref[h, :, LANE:]
        d = o[:tq] - lam * o[tq:]
        y = _rms(d, sub_ref[...], LANE) * (1.0 - lambda_init)
        o_ref[:, _head(h)] = (y * _silu(gate_ref[:, _head(h)].astype(F32))).astype(o_ref.dtype)


def _da_branch(p, tabs, qg, kg, lamv, subln, lambda_init, bsz, s):
    t = bsz * s
    tq = min(DA_TQ, s)
    nq = s // tq
    cos, sin = tabs
    cq, ck, cv, cg = (_COL[n] // BRANCH for n in ('da_q', 'da_k', 'da_v', 'da_gate'))
    row = lambda b, i: b * nq + i
    one = lambda b, i: (0, 0)
    kern = functools.partial(_da_kernel, tq=tq, lambda_init=lambda_init)
    map_id = jnp.arange(2 * LANE) // DA_DHALF
    map_sum = (map_id[:, None] == map_id[None, :]).astype(_MXU)
    return pl.pallas_call(
        kern,
        grid=(bsz, nq),
        in_specs=[
            pl.BlockSpec((tq, BRANCH), lambda b, i: (row(b, i), cq)),
            pl.BlockSpec((s, BRANCH), lambda b, i: (b, ck)),
            pl.BlockSpec((s, BRANCH), lambda b, i: (b, cv)),
            pl.BlockSpec((tq, BRANCH), lambda b, i: (row(b, i), cg)),
            pl.BlockSpec((tq, LANE), lambda b, i: (row(b, i), 0)),
            pl.BlockSpec((tq, LANE), lambda b, i: (row(b, i), 0)),
            pl.BlockSpec((s, LANE), lambda b, i: (b, 0)),
            pl.BlockSpec((s, LANE), lambda b, i: (b, 0)),
            pl.BlockSpec((1, LANE), one),
            pl.BlockSpec((1, LANE), one),
            pl.BlockSpec((1, LANE), one),
            pl.BlockSpec((4, DA_DHALF), one),
            pl.BlockSpec((2 * LANE, 2 * LANE), one),
        ],
        out_specs=pl.BlockSpec((tq, BRANCH), lambda b, i: (row(b, i), 0)),
        out_shape=jax.ShapeDtypeStruct((t, BRANCH), _ACT),
        scratch_shapes=[
            pltpu.VMEM((s, BRANCH), _MXU),
            pltpu.VMEM((s, DA_HEADS * 2 * LANE), _MXU),
            pltpu.VMEM((DA_HEADS, 2 * tq, LANE), _MXU),
            pltpu.VMEM((DA_HEADS, 2 * tq, LANE), F32),
            pltpu.VMEM((DA_HEADS, 2 * tq, 2 * LANE), F32),
        ],
        compiler_params=_params(2),
        name="diff_attention",
    )(p, p, p, p, cos, sin, cos, sin, qg, kg, subln, lamv, map_sum)


def _mla_kernel(cq_ref, ckv_ref, kr_ref, gate_ref, cosq_ref, sinq_ref, cosk_ref, sink_ref,
                qan_ref, kvan_ref, wuq_ref, wukv_ref, qn_ref, kn_ref, o_ref,
                kp_ref, vp_ref, qq_ref, m_ref, acc_ref, *, tq):
    qi = pl.program_id(1)
    dk = 2 * LANE
    lane = lax.broadcasted_iota(jnp.int32, (1, LANE), 1)
    rot_lo = lane < MLA_ROPE // 2

    def rope(x, cos, sin):
        x_sw = jnp.where(rot_lo, pltpu.roll(x, LANE - MLA_ROPE // 2, 1), pltpu.roll(x, MLA_ROPE // 2, 1))
        return x * cos + x_sw * sin

    @pl.when(qi == 0)
    def _():
        c = _rms(ckv_ref[...].astype(F32), kvan_ref[...], MLA_KV_RANK).astype(_MXU)
        k_rope = jnp.where(lane < MLA_ROPE, kr_ref[...].astype(F32), 0.0)
        ss_rope = jnp.sum(k_rope * k_rope, axis=-1, keepdims=True)
        gain = kn_ref[...]
        k_rot = rope(k_rope * gain[:, MLA_NOPE:], cosk_ref[...], sink_ref[...])
        for h in range(MLA_HEADS):
            kv = jnp.dot(c, wukv_ref[:, _head(h, dk)], preferred_element_type=F32)
            k_nope = kv[:, :MLA_NOPE]
            ss = jnp.sum(k_nope * k_nope, axis=-1, keepdims=True) + ss_rope
            r = lax.rsqrt(ss * (1.0 / MLA_QK) + EPS)
            kp_ref[:, h * dk:h * dk + MLA_NOPE] = (k_nope * r * gain[:, :MLA_NOPE]).astype(_MXU)
            kp_ref[:, h * dk + MLA_NOPE:(h + 1) * dk] = (k_rot * r).astype(_MXU)
            vp_ref[:, 2 * h * LANE:(2 * h + 1) * LANE] = kv[:, MLA_NOPE:].astype(_MXU)
            vp_ref[:, (2 * h + 1) * LANE:(2 * h + 2) * LANE] = jnp.ones((vp_ref.shape[0], LANE), _MXU)

    c = _rms(cq_ref[...].astype(F32), qan_ref[...], MLA_Q_RANK).astype(_MXU)
    for h in range(MLA_HEADS):
        q = jnp.dot(c, wuq_ref[:, _head(h, dk)], preferred_element_type=F32)
        q = _rms(q, qn_ref[...], MLA_QK) * (MLA_QK ** -0.5 * LOG2E)
        qq_ref[h] = jnp.concatenate([q[:, :MLA_NOPE], rope(q[:, MLA_NOPE:], cosq_ref[...], sinq_ref[...])],
                                    axis=1).astype(_MXU)
    _flash(qq_ref, kp_ref, vp_ref, m_ref, acc_ref, qi, tq, dk)
    for h in range(MLA_HEADS):
        o = acc_ref[h, :, :LANE] / acc_ref[h, :, LANE:]
        o_ref[:, _head(h)] = (o * _silu(gate_ref[:, _head(h)].astype(F32))).astype(o_ref.dtype)


def _mla_branch(p, tabs, qan, kvan, wuq, wukv, qn, kn, bsz, s):
    t = bsz * s
    tq = min(MLA_TQ, s)
    nq = s // tq
    cos, sin = tabs
    ccq = _COL['mla_cq'] // MLA_Q_RANK
    cckv, ckr = (_COL[n] // LANE for n in ('mla_ckv', 'mla_krope'))
    cg = _COL['mla_gate'] // BRANCH
    row = lambda b, i: b * nq + i
    one = lambda b, i: (0, 0)
    dk = 2 * LANE
    return pl.pallas_call(
        functools.partial(_mla_kernel, tq=tq),
        grid=(bsz, nq),
        in_specs=[
            pl.BlockSpec((tq, MLA_Q_RANK), lambda b, i: (row(b, i), ccq)),
            pl.BlockSpec((s, LANE), lambda b, i: (b, cckv)),
            pl.BlockSpec((s, LANE), lambda b, i: (b, ckr)),
            pl.BlockSpec((tq, BRANCH), lambda b, i: (row(b, i), cg)),
            pl.BlockSpec((tq, LANE), lambda b, i: (row(b, i), 0)),
            pl.BlockSpec((tq, LANE), lambda b, i: (row(b, i), 0)),
            pl.BlockSpec((s, LANE), lambda b, i: (b, 0)),
            pl.BlockSpec((s, LANE), lambda b, i: (b, 0)),
            pl.BlockSpec((1, MLA_Q_RANK), one),
            pl.BlockSpec((1, MLA_KV_RANK), one),
            pl.BlockSpec((MLA_Q_RANK, MLA_HEADS * dk), one),
            pl.BlockSpec((MLA_KV_RANK, MLA_HEADS * dk), one),
            pl.BlockSpec((1, dk), one),
            pl.BlockSpec((1, dk), one),
        ],
        out_specs=pl.BlockSpec((tq, BRANCH), lambda b, i: (row(b, i), 0)),
        out_shape=jax.ShapeDtypeStruct((t, BRANCH), _ACT),
        scratch_shapes=[
            pltpu.VMEM((s, MLA_HEADS * dk), _MXU),
            pltpu.VMEM((s, MLA_HEADS * 2 * LANE), _MXU),
            pltpu.VMEM((MLA_HEADS, tq, dk), _MXU),
            pltpu.VMEM((MLA_HEADS, tq, LANE), F32),
            pltpu.VMEM((MLA_HEADS, tq, 2 * LANE), F32),
        ],
        compiler_params=_params(2),
        name="mla_attention",
    )(p, p, p, p, cos, sin, cos, sin, qan, kvan, wuq, wukv, qn, kn)


def _mamba_kernel(z_ref, xbc_ref, dt_ref, cw_ref, cb_ref, dtb_ref, alog_ref, d_ref, ng_ref,
                  o_ref, xb_ref, y_ref, st_ref, *, lc):
    halo = 8

    @pl.when(pl.program_id(1) == 0)
    def _():
        xb_ref[0:halo, :] = jnp.zeros((halo, xb_ref.shape[1]), F32)
        st_ref[...] = jnp.zeros(st_ref.shape, F32)

    xb_ref[halo:halo + lc, :] = xbc_ref[...].astype(F32)
    conv = cb_ref[...]
    for k in range(MB_CONV):
        conv = conv + cw_ref[k:k + 1, :] * xb_ref[pl.ds(halo - MB_CONV + 1 + k, lc), :]
    xb_ref[0:halo, :] = xb_ref[lc:lc + halo, :]
    xa = _silu(conv)
    xs = xa[:, :BRANCH]
    bm = xa[:, BRANCH:BRANCH + 2 * MB_STATE]
    cm = xa[:, BRANCH + 2 * MB_STATE:]

    dt = jax.nn.softplus(dt_ref[...] + dtb_ref[...])
    a = dt * (-jnp.exp(alog_ref[...]))
    r_i = lax.broadcasted_iota(jnp.int32, (lc, lc), 0)
    c_i = lax.broadcasted_iota(jnp.int32, (lc, lc), 1)
    tril = r_i >= c_i
    cs = jnp.dot(tril.astype(F32), a, preferred_element_type=F32,
                 precision=lax.Precision.HIGHEST)
    cs_t = cs.T
    cs_last = cs[lc - 1:lc, :]

    lane = lax.broadcasted_iota(jnp.int32, (1, LANE), 1)
    left = lane < 64
    top = lax.broadcasted_iota(jnp.int32, (LANE, 1), 0) < 64

    def head_col(arr, h):
        return arr[:, MB_DT_LANE + h:MB_DT_LANE + h + 1]

    def pair(arr, h0):
        return jnp.where(left, head_col(arr, h0), head_col(arr, h0 + 1))

    for g in range(2):
        bg = bm[:, g * MB_STATE:(g + 1) * MB_STATE].astype(_MXU)
        cg = cm[:, g * MB_STATE:(g + 1) * MB_STATE].astype(_MXU)
        cb = lax.dot_general(cg, bg, (((1,), (1,)), ((), ())), preferred_element_type=F32)
        for pp in range(2):
            pr = 2 * g + pp
            h0 = 2 * pr
            xp = xs[:, pr * LANE:(pr + 1) * LANE]
            xdt = xp * pair(dt, h0)
            cse = pair(cs, h0)
            xdt_m = xdt.astype(_MXU)
            ys = []
            for h in (h0, h0 + 1):
                seg = head_col(cs, h) - cs_t[MB_DT_LANE + h:MB_DT_LANE + h + 1, :]
                lm = jnp.exp(jnp.where(tril, seg, -jnp.inf))
                ys.append(jnp.dot((cb * lm).astype(_MXU), xdt_m, preferred_element_type=F32))
            y = jnp.where(left, ys[0], ys[1])
            st = st_ref[pr]
            y = y + jnp.exp(cse) * lax.dot_general(cg, st.astype(_MXU), (((1,), (1,)), ((), ())),
                                                   preferred_element_type=F32)
            w_t = (xdt * jnp.exp(pair(cs_last, h0) - cse)).T.astype(_MXU)
            new = jnp.dot(w_t, bg, preferred_element_type=F32)
            dec = jnp.exp(jnp.where(top, head_col(cs_last, h0), head_col(cs_last, h0 + 1)))
            st_ref[pr] = st * dec + new
            y = y + d_ref[:, pr * LANE:(pr + 1) * LANE] * xp
            y = y * _silu(z_ref[:, pr * LANE:(pr + 1) * LANE].astype(F32))
            y_ref[:, pr * LANE:(pr + 1) * LANE] = y

    gw = BRANCH // 2
    for g in range(2):
        sl = slice(g * gw, (g + 1) * gw)
        o_ref[:, sl] = _rms(y_ref[:, sl], ng_ref[:, sl], gw).astype(o_ref.dtype)


def _mamba_branch(p, dt, conv_w, conv_b, dt_bias, a_log, d_exp, norm_g, bsz, s):
    t = bsz * s
    lc = min(MB_CHUNK, s)
    nc = s // lc
    cz = _COL['mb_z'] // BRANCH
    cx = _COL['mb_xbc'] // 1024
    row = lambda b, c: b * nc + c
    one = lambda b, c: (0, 0)
    return pl.pallas_call(
        functools.partial(_mamba_kernel, lc=lc),
        grid=(bsz, nc),
        in_specs=[
            pl.BlockSpec((lc, BRANCH), lambda b, c: (row(b, c), cz)),
            pl.BlockSpec((lc, 1024), lambda b, c: (row(b, c), cx)),
            pl.BlockSpec((lc, LANE), lambda b, c: (row(b, c), 0)),
            pl.BlockSpec((MB_CONV, 1024), one),
            pl.BlockSpec((1, 1024), one),
            pl.BlockSpec((1, LANE), one),
            pl.BlockSpec((1, LANE), one),
            pl.BlockSpec((1, BRANCH), one),
            pl.BlockSpec((1, BRANCH), one),
        ],
        out_specs=pl.BlockSpec((lc, BRANCH), lambda b, c: (row(b, c), 0)),
        out_shape=jax.ShapeDtypeStruct((t, BRANCH), _ACT),
        scratch_shapes=[
            pltpu.VMEM((lc + 8, 1024), F32),
            pltpu.VMEM((lc, BRANCH), F32),
            pltpu.VMEM((4, LANE, MB_STATE), F32),
        ],
        compiler_params=_params(2),
        name="mamba_ssd",
    )(p, p, dt, conv_w, conv_b, dt_bias, a_log, d_exp, norm_g)


def _s5_param_kernel(pre_ref, pim_ref, qre_ref, qim_ref, rre_ref, rim_ref, bre_ref, bim_ref,
                     cre_ref, cim_ref, kq_ref, wst_ref, wout_ref, mre_ref, mim_ref):
    l, w = S5_CHUNK, S5_GROUP
    hi = lax.Precision.HIGHEST
    nt = (((1,), (1,)), ((), ()))
    p_re, p_im = pre_ref[...], pim_ref[...]
    q_re, q_im = qre_ref[...], qim_ref[...]
    r_re, r_im = rre_ref[...], rim_ref[...]
    c_re, c_im = cre_ref[...], cim_ref[...]
    for ci in range(w):
        b_re, b_im = bre_ref[ci:ci + 1, :], bim_ref[ci:ci + 1, :]
        mre_ref[ci * l:(ci + 1) * l, :] = p_re * b_re - p_im * b_im
        mim_ref[ci * l:(ci + 1) * l, :] = p_re * b_im + p_im * b_re
        s_re = r_re * b_re - r_im * b_im
        s_im = r_re * b_im + r_im * b_re
        wst_ref[ci * l:(ci + 1) * l, :] = jnp.concatenate([s_re, s_im, s_im, s_re], axis=1).astype(_MXU)
    kq_ref[...] = (lax.dot_general(mre_ref[...], c_re, nt, precision=hi, preferred_element_type=F32)
                   - lax.dot_general(mim_ref[...], c_im, nt, precision=hi, preferred_element_type=F32))
    for co in range(w):
        g_re, g_im = c_re[co:co + 1, :], c_im[co:co + 1, :]
        n_re = q_re * g_re - q_im * g_im
        n_im = -(q_re * g_im + q_im * g_re)
        wout_ref[:, co * l:(co + 1) * l] = jnp.concatenate([n_re, n_im], axis=1).T.astype(_MXU)


def _s5_kernel(u_ref, kvec_ref, knext_ref, wst_ref, wout_ref, ar_ref, ai_ref, y_ref,
               toep_a, toep_b, u2_ref, sa_ref, sb_ref, hin_ref, *, nb, nc):
    l, w = S5_CHUNK, S5_GROUP
    i = pl.program_id(0)
    keep = lax.broadcasted_iota(jnp.int32, (l, l), 1) >= lax.broadcasted_iota(jnp.int32, (l, l), 0)

    def build(src_ref, toep_ref):
        for ci in range(w):
            for co in range(w):
                kv = jnp.broadcast_to(src_ref[ci * w + co:ci * w + co + 1, :], (l, l))
                blk = pltpu.roll(kv, 0, 1, stride=1, stride_axis=0)
                toep_ref[ci * l:(ci + 1) * l, co * l:(co + 1) * l] = jnp.where(keep, blk, 0.0).astype(_MXU)

    def group(toep_cur, toep_next):
        for ci in range(w):
            u2_ref[:, ci * l:(ci + 1) * l] = u_ref[ci]
        u = u2_ref[...]
        s2 = jnp.dot(u, wst_ref[...], preferred_element_type=F32)
        sa_ref[...] = s2[:, :LANE]
        sb_ref[...] = s2[:, LANE:]
        ar = ar_ref[...]
        ai = ai_ref[...]
        h = jnp.zeros((nb, LANE), F32)
        hs = h
        for c in range(nc):
            rows_c = pl.ds(c, nb, stride=nc)
            hin_ref[rows_c, :] = h
            h, hs = h * ar + hs * ai + sa_ref[rows_c, :], hs * ar - h * ai + sb_ref[rows_c, :]
        build(knext_ref, toep_next)
        y = jnp.dot(u, toep_cur[...], preferred_element_type=F32)
        y = y + jnp.dot(hin_ref[...].astype(_MXU), wout_ref[...], preferred_element_type=F32)
        for co in range(w):
            y_ref[co] = y[:, co * l:(co + 1) * l].astype(y_ref.dtype)

    @pl.when(i == 0)
    def _():
        build(kvec_ref, toep_a)

    @pl.when(i % 2 == 0)
    def _():
        group(toep_a, toep_b)

    @pl.when(i % 2 == 1)
    def _():
        group(toep_b, toep_a)


def _s5_weights(lam_re, lam_im, log_step, b_re, b_im, c_re, c_im):
    g, pn, l, w = S5_GROUPS, S5_STATE, S5_CHUNK, S5_GROUP
    lr, li = lam_re.astype(F32), lam_im.astype(F32)
    step = jnp.exp(log_step.astype(F32))[:, None]
    mag = jnp.exp(lr * step)
    ab_re, ab_im = mag * jnp.cos(li * step), mag * jnp.sin(li * step)
    den = lr * lr + li * li
    f_re = ((ab_re - 1.0) * lr + ab_im * li) / den
    f_im = (ab_im * lr - (ab_re - 1.0) * li) / den
    br, bi = b_re.astype(F32), b_im.astype(F32)
    bb_re = f_re[..., None] * br - f_im[..., None] * bi
    bb_im = f_re[..., None] * bi + f_im[..., None] * br
    d = jnp.arange(l + 1, dtype=F32)[None, :, None]
    pmag = jnp.exp(d * (lr * step)[:, None, :])
    pw_re = pmag * jnp.cos(d * (li * step)[:, None, :])
    pw_im = pmag * jnp.sin(d * (li * step)[:, None, :])
    bt_re, bt_im = jnp.swapaxes(bb_re, 1, 2), jnp.swapaxes(bb_im, 1, 2)

    per = lambda *shape: pl.BlockSpec((None,) + shape, lambda i: (i, 0, 0))
    kq, wst, wout = pl.pallas_call(
        _s5_param_kernel,
        grid=(g,),
        in_specs=[per(l, pn)] * 6 + [per(w, pn)] * 4,
        out_specs=[per(w * l, w), per(w * l, 2 * LANE), per(LANE, w * l)],
        out_shape=[jax.ShapeDtypeStruct((g, w * l, w), F32),
                   jax.ShapeDtypeStruct((g, w * l, 2 * LANE), _MXU),
                   jax.ShapeDtypeStruct((g, LANE, w * l), _MXU)],
        scratch_shapes=[pltpu.VMEM((w * l, pn), F32), pltpu.VMEM((w * l, pn), F32)],
        compiler_params=_params(1),
        name="s5_params",
    )(pw_re[:, :l], pw_im[:, :l], pw_re[:, 1:], pw_im[:, 1:], pw_re[:, l - 1::-1], pw_im[:, l - 1::-1],
      bt_re, bt_im, c_re.astype(F32), c_im.astype(F32))
    kvec = kq.reshape(g, w, l, w).transpose(0, 1, 3, 2).reshape(g, w * w, l)
    ar = jnp.concatenate([pw_re[:, l], pw_re[:, l]], axis=-1)[:, None, :]
    ai = jnp.concatenate([-pw_im[:, l], pw_im[:, l]], axis=-1)[:, None, :]
    return kvec, wst, wout, ar, ai


def _s5_scan(p, weights, bsz, s):
    kvec, wst, wout, ar, ai = weights
    g, l, w = S5_GROUPS, S5_CHUNK, S5_GROUP
    t = bsz * s
    nc = s // l
    rows = bsz * nc
    u_t = lax.slice_in_dim(p, _COL['s5_u'], _COL['s5_u'] + BRANCH, axis=1).T.reshape(BRANCH, rows, l)
    per = lambda *shape: pl.BlockSpec((None,) + shape, lambda i: (i, 0, 0))
    chan = pl.BlockSpec((w, rows, l), lambda i: (i, 0, 0))
    nxt = pl.BlockSpec((None, w * w, l), lambda i: (jnp.minimum(i + 1, g - 1), 0, 0))
    y_t = pl.pallas_call(
        functools.partial(_s5_kernel, nb=bsz, nc=nc),
        grid=(g,),
        in_specs=[chan, per(w * w, l), nxt, per(w * l, 2 * LANE), per(LANE, w * l), per(1, LANE), per(1, LANE)],
        out_specs=chan,
        out_shape=jax.ShapeDtypeStruct((BRANCH, rows, l), _ACT),
        scratch_shapes=[
            pltpu.VMEM((w * l, w * l), _MXU),
            pltpu.VMEM((w * l, w * l), _MXU),
            pltpu.VMEM((rows, w * l), _MXU),
            pltpu.VMEM((rows, LANE), F32),
            pltpu.VMEM((rows, LANE), F32),
            pltpu.VMEM((rows, LANE), F32),
        ],
        compiler_params=_params(1),
        name="s5_scan",
    )(u_t, kvec, kvec, wst, wout, ar, ai)
    return y_t.reshape(BRANCH, t).T


def _merge_kernel(ya_ref, yb_ref, ys_ref, yd_ref, u_ref, sg_ref, gates_ref, x_ref,
                  sd_ref, wglu_ref, bglu_ref, wbr_ref, wout_ref, o_ref):
    y = ys_ref[...].astype(F32) + sd_ref[...] * u_ref[...].astype(F32)
    y = 0.5 * y * (1.0 + jnp.tanh(math.sqrt(2.0 / math.pi) * (y + 0.044715 * (y * y * y))))
    y = y * jax.nn.sigmoid(jnp.dot(y.astype(_MXU), wglu_ref[...], preferred_element_type=F32) + bglu_ref[...])
    yc = y * _silu(sg_ref[...].astype(F32))
    branches = (ya_ref[...].astype(_MXU), yb_ref[...].astype(_MXU), yc.astype(_MXU), yd_ref[...].astype(_MXU))
    merged = None
    for n, br in enumerate(branches):
        gate = jax.nn.sigmoid(gates_ref[:, n * D_MODEL:(n + 1) * D_MODEL].astype(F32))
        term = gate * jnp.dot(br, wbr_ref[n], preferred_element_type=F32)
        merged = term if merged is None else merged + term
    o_ref[...] = x_ref[...] + jnp.dot(merged.astype(_MXU), wout_ref[...], preferred_element_type=F32)


def _merge(ya, yb, ys, yd, p, x2d, s5_d, w_glu, b_glu, w_br, w_out):
    t = x2d.shape[0]
    tm = min(512, t)
    cu, csg = _COL['s5_u'] // BRANCH, _COL['s5_gate'] // BRANCH
    br = pl.BlockSpec((tm, BRANCH), lambda i: (i, 0))
    return pl.pallas_call(
        _merge_kernel,
        grid=(t // tm,),
        in_specs=[
            br, br, br, br,
            pl.BlockSpec((tm, BRANCH), lambda i: (i, cu)),
            pl.BlockSpec((tm, BRANCH), lambda i: (i, csg)),
            pl.BlockSpec((tm, 4 * D_MODEL), lambda i: (i, 0)),
            pl.BlockSpec((tm, D_MODEL), lambda i: (i, 0)),
            pl.BlockSpec((1, BRANCH), lambda i: (0, 0)),
            pl.BlockSpec((BRANCH, BRANCH), lambda i: (0, 0)),
            pl.BlockSpec((1, BRANCH), lambda i: (0, 0)),
            pl.BlockSpec((4, BRANCH, D_MODEL), lambda i: (0, 0, 0)),
            pl.BlockSpec((D_MODEL, D_MODEL), lambda i: (0, 0)),
        ],
        out_specs=pl.BlockSpec((tm, D_MODEL), lambda i: (i, 0)),
        out_shape=jax.ShapeDtypeStruct((t, D_MODEL), F32),
        compiler_params=_params(1),
        name="merge_out",
    )(ya, yb, ys, yd, p, p, p, x2d, s5_d, w_glu, b_glu, w_br, w_out)


def _pack_w_in(w):
    cols = [lax.slice_in_dim(w, _SRC_OFF[n], _SRC_OFF[n] + k, axis=1) for n, k in _DST_LAYOUT]
    cols.append(jnp.zeros((D_MODEL, NP - _NP_USED), w.dtype))
    return jnp.concatenate(cols, axis=1).astype(_MXU)


def _rope_tables(positions, rot, width, period):
    half = rot // 2
    inv_freq = 1.0 / (ROPE_THETA ** (jnp.arange(0, rot, 2, dtype=F32) / rot))
    ang = positions.astype(F32).reshape(-1, 1) * inv_freq
    cos, sin = jnp.cos(ang), jnp.sin(ang)
    t = ang.shape[0]
    pad = period - rot
    cos_p = jnp.concatenate([cos, cos, jnp.ones((t, pad), F32)], axis=1)
    sin_p = jnp.concatenate([-sin, sin, jnp.zeros((t, pad), F32)], axis=1)
    reps = width // period
    cos_p, sin_p = jnp.tile(cos_p, (1, reps)), jnp.tile(sin_p, (1, reps))
    if width < LANE:
        cos_p = jnp.concatenate([cos_p, jnp.ones((t, LANE - width), F32)], axis=1)
        sin_p = jnp.concatenate([sin_p, jnp.zeros((t, LANE - width), F32)], axis=1)
    return cos_p, sin_p


def _row(v, width=None, offset=0):
    v = v.astype(F32).reshape(1, -1)
    if width is not None and v.shape[1] < width:
        v = jnp.pad(v, ((0, 0), (offset, width - offset - v.shape[1])))
    return v


def kernel(x, positions, norm_g, w_in, da_q_norm, da_k_norm, da_lambda_q1, da_lambda_k1,
           da_lambda_q2, da_lambda_k2, da_subln, mb_conv_w, mb_conv_b, mb_dt_bias, mb_a_log,
           mb_d, mb_norm, s5_lam_re, s5_lam_im, s5_log_step, s5_b_re, s5_b_im, s5_c_re,
           s5_c_im, s5_d, s5_w_glu, s5_b_glu, mla_q_a_norm, mla_w_uq, mla_kv_a_norm,
           mla_w_ukv, mla_q_norm, mla_k_norm, w_br, w_out):
    bsz, s, _ = x.shape
    depth = w_in.shape[0]
    x2d = x.reshape(bsz * s, D_MODEL)
    da_tabs = _rope_tables(positions, DA_ROT, LANE, DA_DHALF)
    mla_tabs = _rope_tables(positions, MLA_ROPE, MLA_ROPE, MLA_ROPE)
    for l in range(depth):
        lambda_init = 0.8 - 0.6 * math.exp(-0.3 * l)
        p, dt = _in_proj(x2d, norm_g[l], _pack_w_in(w_in[l]))

        lamv = jnp.stack([da_lambda_q1[l], da_lambda_k1[l], da_lambda_q2[l], da_lambda_k2[l]]).astype(F32)
        y_a = _da_branch(p, da_tabs, _row(jnp.tile(da_q_norm[l], 2)), _row(jnp.tile(da_k_norm[l], 2)),
                         lamv, _row(da_subln[l]), lambda_init, bsz, s)

        y_b = _mamba_branch(p, dt, mb_conv_w[l].astype(F32), _row(mb_conv_b[l]),
                            _row(mb_dt_bias[l], LANE, MB_DT_LANE), _row(mb_a_log[l], LANE, MB_DT_LANE),
                            _row(jnp.repeat(mb_d[l], BRANCH // MB_HEADS)), _row(mb_norm[l]), bsz, s)

        s5w = _s5_weights(s5_lam_re[l], s5_lam_im[l], s5_log_step[l], s5_b_re[l], s5_b_im[l],
                          s5_c_re[l], s5_c_im[l])
        y_s = _s5_scan(p, s5w, bsz, s)

        wuq = mla_w_uq[l].reshape(MLA_Q_RANK, MLA_HEADS, MLA_QK)
        wuq = jnp.pad(wuq, ((0, 0), (0, 0), (0, 2 * LANE - MLA_QK))).reshape(MLA_Q_RANK, -1).astype(_MXU)
        wukv = mla_w_ukv[l].astype(_MXU)
        y_d = _mla_branch(p, mla_tabs, _row(mla_q_a_norm[l]), _row(mla_kv_a_norm[l]), wuq, wukv,
                          _row(mla_q_norm[l], 2 * LANE), _row(mla_k_norm[l], 2 * LANE), bsz, s)

        x2d = _merge(y_a, y_b, y_s, y_d, p, x2d, _row(s5_d[l]), s5_w_glu[l].astype(_MXU),
                     _row(s5_b_glu[l]), w_br[l].astype(_MXU), w_out[l].astype(_MXU))
    return x2d.reshape(bsz, s, D_MODEL)
```
